```python
import math
import jax, jax.numpy as jnp
from jax import lax
import numpy as np

D_MODEL = 1024
BATCH = 2
SEQ = 8192
DEPTH = 2

HEAD_DIM = 64
MOBA_HEADS = 8
MOBA_WIDTH = MOBA_HEADS * HEAD_DIM
MOBA_BLOCK = 256
MOBA_TOPK = 3
MOBA_QCHUNK = 128
RWKV_HEADS = 8
RWKV_WIDTH = RWKV_HEADS * HEAD_DIM
DECAY_LORA = 64
ICLR_LORA = 64
GN_EPS = HEAD_DIM * 1e-5
DECAY_SCALE = math.exp(-0.5)
SWA_Q_HEADS = 8
SWA_KV_HEADS = 2
SWA_GROUP = SWA_Q_HEADS // SWA_KV_HEADS
SWA_WIDTH = SWA_Q_HEADS * HEAD_DIM
SWA_KV_WIDTH = SWA_KV_HEADS * HEAD_DIM
WINDOW = 128
SWA_BLOCK = 128
ROPE_THETA = 10000.0
LN_EPS = 1e-5
ALPHA = (2 * DEPTH) ** 0.25
BETA = (8 * DEPTH) ** -0.25

A_SPLITS = (MOBA_WIDTH, MOBA_WIDTH, MOBA_WIDTH, MOBA_WIDTH)
B_SPLITS = (RWKV_WIDTH, RWKV_WIDTH, RWKV_WIDTH, DECAY_LORA, ICLR_LORA, RWKV_WIDTH)
C_SPLITS = (SWA_WIDTH, SWA_KV_WIDTH, SWA_KV_WIDTH, SWA_WIDTH)
G_SPLITS = (D_MODEL, D_MODEL, D_MODEL)
GROUP_SPLITS = (sum(A_SPLITS), sum(B_SPLITS), sum(C_SPLITS), sum(G_SPLITS))
IN_WIDTH = sum(GROUP_SPLITS)
RWKV_IN_WIDTH = sum(B_SPLITS)

kernel_name = 'hybrid_moba_rwkv7_swa_sink_gated'


def _split(h, sizes):
    offs = np.cumsum(sizes)[:-1].tolist()
    return jnp.split(h, offs, axis=-1)


def layer_norm(x, g, b):
    xf = x.astype(jnp.float32)
    mu = xf.mean(-1, keepdims=True)
    var = jnp.square(xf - mu).mean(-1, keepdims=True)
    return ((xf - mu) * lax.rsqrt(var + LN_EPS) * g.astype(jnp.float32) + b.astype(jnp.float32)).astype(x.dtype)


def rope(x, positions):
    half = HEAD_DIM // 2
    inv = jnp.power(ROPE_THETA, -jnp.arange(half, dtype=jnp.float32) / half)
    ang = positions.astype(jnp.float32)[..., None] * inv
    cos = jnp.cos(ang)[:, :, None, :]
    sin = jnp.sin(ang)[:, :, None, :]
    xf = x.astype(jnp.float32)
    x1, x2 = xf[..., :half], xf[..., half:]
    return jnp.concatenate([x1 * cos - x2 * sin, x2 * cos + x1 * sin], axis=-1).astype(x.dtype)


def moba_attention(q, k, v):
    B, S, H, D = q.shape
    s_pad = -(-S // MOBA_BLOCK) * MOBA_BLOCK
    nb = s_pad // MOBA_BLOCK
    topk = min(MOBA_TOPK, nb)
    pad = ((0, 0), (0, s_pad - S), (0, 0), (0, 0))
    kb = jnp.pad(k, pad).reshape(B, nb, MOBA_BLOCK, H, D).transpose(0, 3, 1, 2, 4)
    vb = jnp.pad(v, pad).reshape(B, nb, MOBA_BLOCK, H, D).transpose(0, 3, 1, 2, 4)
    kmean = kb.astype(jnp.float32).mean(axis=3)
    qh = q.transpose(0, 2, 1, 3)
    bi = jnp.arange(B)[:, None, None, None]
    hi = jnp.arange(H)[None, :, None, None]
    blk_ids = jnp.arange(nb)
    offs = jnp.arange(MOBA_BLOCK)
    scale = D ** -0.5

    def chunk(c):
        q0 = c * MOBA_QCHUNK
        qc = lax.dynamic_slice_in_dim(qh, q0, MOBA_QCHUNK, axis=2)
        qpos = q0 + jnp.arange(MOBA_QCHUNK)
        own = q0 // MOBA_BLOCK
        gate = jnp.einsum('bhqd,bhnd->bhqn', qc.astype(jnp.float32), kmean)
        gate = jnp.where(blk_ids < own, gate, -jnp.inf)
        _, idx = lax.top_k(gate, topk)
        valid = idx < own
        ksel = kb[bi, hi, idx]
        vsel = vb[bi, hi, idx]
        s_sel = jnp.einsum('bhqd,bhqnkd->bhqnk', qc, ksel).astype(jnp.float32) * scale
        s_sel = jnp.where(valid[..., None], s_sel, -jnp.inf).reshape(B, H, MOBA_QCHUNK, topk * MOBA_BLOCK)
        kown = lax.dynamic_index_in_dim(kb, own, axis=2, keepdims=False)
        vown = lax.dynamic_index_in_dim(vb, own, axis=2, keepdims=False)
        s_own = jnp.einsum('bhqd,bhkd->bhqk', qc, kown).astype(jnp.float32) * scale
        kpos = own * MOBA_BLOCK + offs
        s_own = jnp.where(kpos[None, :] <= qpos[:, None], s_own, -jnp.inf)
        p = jax.nn.softmax(jnp.concatenate([s_sel, s_own], axis=-1), axis=-1).astype(v.dtype)
        p_sel = p[..., :topk * MOBA_BLOCK].reshape(B, H, MOBA_QCHUNK, topk, MOBA_BLOCK)
        p_own = p[..., topk * MOBA_BLOCK:]
        return (jnp.einsum('bhqnk,bhqnkd->bhqd', p_sel, vsel)
                + jnp.einsum('bhqk,bhkd->bhqd', p_own, vown))

    out = lax.map(chunk, jnp.arange(S // MOBA_QCHUNK))
    return out.transpose(1, 0, 3, 2, 4).reshape(B, S, H * D)


def swa_sink_attention(q, k, v, sinks):
    B, S, _, D = q.shape
    nb = S // SWA_BLOCK
    qb = q.reshape(B, nb, SWA_BLOCK, SWA_KV_HEADS, SWA_GROUP, D)
    pad = ((0, 0), (SWA_BLOCK, 0), (0, 0), (0, 0))
    kp = jnp.pad(k, pad).reshape(B, nb + 1, SWA_BLOCK, SWA_KV_HEADS, D)
    vp = jnp.pad(v, pad).reshape(B, nb + 1, SWA_BLOCK, SWA_KV_HEADS, D)
    kband = jnp.concatenate([kp[:, :-1], kp[:, 1:]], axis=2)
    vband = jnp.concatenate([vp[:, :-1], vp[:, 1:]], axis=2)
    s = jnp.einsum('bnqhgd,bnkhd->bnhgqk', qb, kband).astype(jnp.float32) * (D ** -0.5)
    qi = jnp.arange(SWA_BLOCK)[:, None]
    ki = jnp.arange(2 * SWA_BLOCK)[None, :] - SWA_BLOCK
    rel = qi - ki
    kabs = jnp.arange(nb)[:, None, None] * SWA_BLOCK + ki[None]
    mask = (rel >= 0)[None] & (rel < WINDOW)[None] & (kabs >= 0)
    s = jnp.where(mask[None, :, None, None], s, -jnp.inf)
    sink = sinks.astype(jnp.float32).reshape(SWA_KV_HEADS, SWA_GROUP)[None, None, :, :, None, None]
    m = jnp.maximum(s.max(-1, keepdims=True), sink)
    p = jnp.exp(s - m)
    denom = p.sum(-1, keepdims=True) + jnp.exp(sink - m)
    o = jnp.einsum('bnhgqk,bnkhd->bnqhgd', (p / denom).astype(v.dtype), vband)
    return o.reshape(B, S, SWA_Q_HEADS * D)


def rwkv7_time_mix(r, k, v, w_lo, a_lo, w0, w_up, a0, a_up, k_k, k_a, r_k, gn_g, gn_b):
    B, S, _ = r.shape
    H, D = RWKV_HEADS, HEAD_DIM
    f = lambda t: t.astype(jnp.float32)
    r, k, v, w_lo, a_lo = f(r), f(k), f(v), f(w_lo), f(a_lo)
    decay = jnp.exp(-DECAY_SCALE * jax.nn.sigmoid(f(w0) + jnp.tanh(w_lo) @ f(w_up)))
    a = jax.nn.sigmoid(f(a0) + a_lo @ f(a_up))
    kk = (k * f(k_k)).reshape(B, S, H, D)
    kk = kk / jnp.maximum(jnp.sqrt(jnp.sum(kk * kk, -1, keepdims=True)), 1e-12)
    k = k * (1.0 + (a - 1.0) * f(k_a))
    hd = lambda t: t.reshape(B, S, H, D)
    r4, k4, v4, w4, a4 = hd(r), hd(k), hd(v), hd(decay), hd(a)
    xs = tuple(jnp.moveaxis(t, 1, 0) for t in (r4, w4, k4, v4, kk, a4))

    def step(state, inp):
        r_t, w_t, k_t, v_t, kk_t, a_t = inp
        sa = jnp.einsum('bhvk,bhk->bhv', state, -kk_t)
        state = (state * w_t[:, :, None, :]
                 + sa[..., None] * (kk_t * a_t)[:, :, None, :]
                 + v_t[..., None] * k_t[:, :, None, :])
        return state, jnp.einsum('bhvk,bhk->bhv', state, r_t)

    state0 = jnp.zeros((B, H, D, D), jnp.float32)
    _, y = lax.scan(step, state0, xs)
    y = jnp.moveaxis(y, 0, 1)
    mu = y.mean(-1, keepdims=True)
    var = jnp.square(y - mu).mean(-1, keepdims=True)
    y = ((y - mu) * lax.rsqrt(var + GN_EPS)).reshape(B, S, H * D) * f(gn_g) + f(gn_b)
    bonus = (jnp.sum(r4 * k4 * f(r_k), -1, keepdims=True) * v4).reshape(B, S, H * D)
    return y + bonus


def hybrid_layer(x, positions, w_in, mu_rwkv, w0, w_up, a0, a_up, k_k, k_a, r_k,
                 gn_g, gn_b, sinks, w_branch_a, w_branch_b, w_branch_c, w_out, ln_g, ln_b):
    B, S, _ = x.shape
    h = x @ w_in
    hA, hB, hC, hG = _split(h, GROUP_SPLITS)
    qa, ka, va, za = _split(hA, A_SPLITS)
    qa = rope(qa.reshape(B, S, MOBA_HEADS, HEAD_DIM), positions)
    ka = rope(ka.reshape(B, S, MOBA_HEADS, HEAD_DIM), positions)
    oa = moba_attention(qa, ka, va.reshape(B, S, MOBA_HEADS, HEAD_DIM))
    hB_prev = jnp.pad(hB, ((0, 0), (1, 0), (0, 0)))[:, :S]
    hB = hB + (hB_prev - hB) * mu_rwkv
    rb, kb, vb, wlo, alo, zb = _split(hB, B_SPLITS)
    ob = rwkv7_time_mix(rb, kb, vb, wlo, alo, w0, w_up, a0, a_up, k_k, k_a, r_k, gn_g, gn_b).astype(x.dtype)
    qc, kc, vc, zc = _split(hC, C_SPLITS)
    qc = rope(qc.reshape(B, S, SWA_Q_HEADS, HEAD_DIM), positions)
    kc = rope(kc.reshape(B, S, SWA_KV_HEADS, HEAD_DIM), positions)
    oc = swa_sink_attention(qc, kc, vc.reshape(B, S, SWA_KV_HEADS, HEAD_DIM), sinks)
    ga, gb, gc = _split(hG, G_SPLITS)
    ya = (oa * jax.nn.silu(za)) @ w_branch_a
    yb = (ob * jax.nn.silu(zb)) @ w_branch_b
    yc = (oc * jax.nn.silu(zc)) @ w_branch_c
    merged = jax.nn.sigmoid(ga) * ya + jax.nn.sigmoid(gb) * yb + jax.nn.sigmoid(gc) * yc
    out = merged @ w_out
    return layer_norm(ALPHA * x + out, ln_g, ln_b)


def setup_inputs(seed: int = 0) -> dict:
    key = jax.random.key(seed)
    ks = jax.random.split(key, 24)
    L = DEPTH
    nrm = lambda k, shape, s: jax.random.normal(k, shape, jnp.float32) * s
    x = jax.random.normal(ks[0], (BATCH, SEQ, D_MODEL), jnp.float32)
    positions = jnp.broadcast_to(jnp.arange(SEQ, dtype=jnp.int32)[None, :], (BATCH, SEQ))
    return {
        'x': x,
        'positions': positions,
        'w_in': nrm(ks[1], (L, D_MODEL, IN_WIDTH), D_MODEL ** -0.5),
        'mu_rwkv': jax.random.uniform(ks[2], (L, RWKV_IN_WIDTH), jnp.float32),
        'w0': nrm(ks[3], (L, RWKV_WIDTH), 1.0),
        'w_up': nrm(ks[4], (L, DECAY_LORA, RWKV_WIDTH), 0.5 * DECAY_LORA ** -0.5),
        'a0': nrm(ks[5], (L, RWKV_WIDTH), 0.5),
        'a_up': nrm(ks[6], (L, ICLR_LORA, RWKV_WIDTH), 0.5 * ICLR_LORA ** -0.5),
        'k_k': 0.85 + nrm(ks[7], (L, RWKV_WIDTH), 0.05),
        'k_a': 1.0 + nrm(ks[8], (L, RWKV_WIDTH), 0.05),
        'r_k': nrm(ks[9], (L, RWKV_HEADS, HEAD_DIM), 0.1),
        'gn_g': 1.0 + nrm(ks[10], (L, RWKV_WIDTH), 0.05),
        'gn_b': nrm(ks[11], (L, RWKV_WIDTH), 0.01),
        'sinks': nrm(ks[12], (L, SWA_Q_HEADS), 1.0),
        'w_branch_a': nrm(ks[13], (L, MOBA_WIDTH, D_MODEL), BETA * MOBA_WIDTH ** -0.5),
        'w_branch_b': nrm(ks[14], (L, RWKV_WIDTH, D_MODEL), BETA * RWKV_WIDTH ** -0.5),
        'w_branch_c': nrm(ks[15], (L, SWA_WIDTH, D_MODEL), BETA * SWA_WIDTH ** -0.5),
        'w_out': nrm(ks[16], (L, D_MODEL, D_MODEL), BETA * D_MODEL ** -0.5),
        'ln_g': 1.0 + nrm(ks[17], (L, D_MODEL), 0.05),
        'ln_b': nrm(ks[18], (L, D_MODEL), 0.01),
    }


def reference(x, positions, w_in, mu_rwkv, w0, w_up, a0, a_up, k_k, k_a, r_k,
              gn_g, gn_b, sinks, w_branch_a, w_branch_b, w_branch_c, w_out, ln_g, ln_b):
    for l in range(DEPTH):
        x = hybrid_layer(x, positions, w_in[l], mu_rwkv[l], w0[l], w_up[l], a0[l], a_up[l],
                         k_k[l], k_a[l], r_k[l], gn_g[l], gn_b[l], sinks[l],
                         w_branch_a[l], w_branch_b[l], w_branch_c[l], w_out[l], ln_g[l], ln_b[l])
    return x
```

```python
import functools
import math

import jax
import jax.numpy as jnp
from jax import lax
from jax.experimental import pallas as pl
from jax.experimental.pallas import tpu as pltpu

F32 = jnp.float32
BF16 = jnp.bfloat16

HEAD_DIM = 64
LANES = 128
MOBA_BLOCK = 256
MOBA_TOPK = 3
MOBA_WIDTH = 512
RWKV_WIDTH = 512
LORA = 64
RWKV_CHUNK = 64
SWA_WIDTH = 512
SWA_KV_WIDTH = 128
SWA_GROUP = 4
SWA_BLOCK = 128
ROPE_THETA = 10000.0
LN_EPS = 1e-5
GN_EPS = HEAD_DIM * 1e-5
DECAY_SCALE = math.exp(-0.5)
NEG = -1e30
VMEM_LIMIT = 56 * 1024 * 1024

A_COLS = 4 * MOBA_WIDTH
B_COLS = 3 * RWKV_WIDTH + 2 * LORA + RWKV_WIDTH
C_COLS = SWA_WIDTH + 2 * SWA_KV_WIDTH + SWA_WIDTH


def _dot(a, b):
    return jnp.dot(a, b, preferred_element_type=F32)


def _dot_nt(a, b):
    return lax.dot_general(a, b, (((1,), (1,)), ((), ())), preferred_element_type=F32)


def _dot_tn(a, b):
    return lax.dot_general(a, b, (((0,), (0,)), ((), ())), preferred_element_type=F32)


def _silu(z):
    return z * jax.nn.sigmoid(z)


def _split3(x):
    hi = x.astype(BF16)
    r1 = x - hi.astype(F32)
    mid = r1.astype(BF16)
    lo = (r1 - mid.astype(F32)).astype(BF16)
    return hi, mid, lo


def _rope_table_kernel(pos_ref, inv_ref, cos_ref, sin_ref):
    ang = pos_ref[...].astype(F32) * inv_ref[...]
    lane = lax.broadcasted_iota(jnp.int32, ang.shape, 1)
    s = jnp.sin(ang)
    cos_ref[...] = jnp.cos(ang)
    sin_ref[...] = jnp.where((lane & 32) == 0, -s, s)


def _rope_tables(pos_col, inv_row, tm=1024):
    t = pos_col.shape[0]
    return pl.pallas_call(
        _rope_table_kernel,
        grid=(t // tm,),
        in_specs=[pl.BlockSpec((tm, 1), lambda i: (i, 0)),
                  pl.BlockSpec((1, LANES), lambda i: (0, 0))],
        out_specs=[pl.BlockSpec((tm, LANES), lambda i: (i, 0))] * 2,
        out_shape=[jax.ShapeDtypeStruct((t, LANES), F32)] * 2,
        name="rope_table",
    )(pos_col, inv_row)


def _proj_kernel(x_ref, cos_ref, sin_ref, wa_ref, wb_ref, wc_ref, mu_ref,
                 qa_ref, ka_ref, vta_ref, za_ref, kmean_ref, hb_ref,
                 qc_ref, kc_ref, vc_ref, zc_ref, carry_ref, *, tiles_per_seq):
    i = pl.program_id(0)
    xb = x_ref[...].astype(BF16)
    tm = xb.shape[0]
    cos = cos_ref[...]
    sin = sin_ref[...]
    lane = lax.broadcasted_iota(jnp.int32, (tm, LANES), 1)
    first_half = (lane & 32) == 0

    def rope(blk):
        partner = jnp.where(first_half, pltpu.roll(blk, 96, 1), pltpu.roll(blk, 32, 1))
        return blk * cos + partner * sin

    scale = HEAD_DIM ** -0.5
    nblk = tm // MOBA_BLOCK

    ha = _dot(xb, wa_ref[...])
    for j in range(MOBA_WIDTH // LANES):
        sl = slice(j * LANES, (j + 1) * LANES)
        qa_ref[:, sl] = (rope(ha[:, sl]) * scale).astype(BF16)
        kj = rope(ha[:, MOBA_WIDTH + j * LANES:MOBA_WIDTH + (j + 1) * LANES])
        ka_ref[:, sl] = kj.astype(BF16)
        for r in range(nblk):
            kmean_ref[r, :, sl] = jnp.mean(kj[r * MOBA_BLOCK:(r + 1) * MOBA_BLOCK], axis=0, keepdims=True)
    va = ha[:, 2 * MOBA_WIDTH:3 * MOBA_WIDTH]
    ones_rows = jnp.ones((16, MOBA_BLOCK), BF16)
    for r in range(nblk):
        vt = va[r * MOBA_BLOCK:(r + 1) * MOBA_BLOCK].T
        for p in range(MOBA_WIDTH // LANES):
            vta_ref[r, p, 0:LANES, :] = vt[p * LANES:(p + 1) * LANES].astype(BF16)
            vta_ref[r, p, LANES:LANES + 16, :] = ones_rows
    za_ref[...] = ha[:, 3 * MOBA_WIDTH:]

    hb = _dot(xb, wb_ref[...])
    prev = pltpu.roll(hb, 1, 0)
    row = lax.broadcasted_iota(jnp.int32, hb.shape, 0)

    @pl.when(i % tiles_per_seq == 0)
    def _():
        carry_ref[...] = jnp.zeros(carry_ref.shape, F32)

    prev = jnp.where(row == 0, carry_ref[...], prev)
    hb_ref[...] = hb + (prev - hb) * mu_ref[...]
    carry_ref[...] = hb[tm - 1:tm, :]

    hc = _dot(xb, wc_ref[...])
    for j in range(SWA_WIDTH // LANES):
        sl = slice(j * LANES, (j + 1) * LANES)
        qc_ref[:, sl] = (rope(hc[:, sl]) * scale).astype(BF16)
    kc_ref[...] = rope(hc[:, SWA_WIDTH:SWA_WIDTH + SWA_KV_WIDTH]).astype(BF16)
    vc_ref[...] = hc[:, SWA_WIDTH + SWA_KV_WIDTH:SWA_WIDTH + 2 * SWA_KV_WIDTH].astype(BF16)
    zc_ref[...] = hc[:, SWA_WIDTH + 2 * SWA_KV_WIDTH:]


def _proj(x2, cos, sin, wa, wb, wc, mu, seq, tm=256):
    t, d = x2.shape
    nb_tile = tm // MOBA_BLOCK
    row = lambda w: pl.BlockSpec((tm, w), lambda i: (i, 0))
    full = lambda a: pl.BlockSpec(a.shape, lambda i: (0,) * a.ndim)
    out_shape = [
        jax.ShapeDtypeStruct((t, MOBA_WIDTH), BF16),
        jax.ShapeDtypeStruct((t, MOBA_WIDTH), BF16),
        jax.ShapeDtypeStruct((t // MOBA_BLOCK, 4, LANES + 16, MOBA_BLOCK), BF16),
        jax.ShapeDtypeStruct((t, MOBA_WIDTH), F32),
        jax.ShapeDtypeStruct((t // MOBA_BLOCK, 1, MOBA_WIDTH), F32),
        jax.ShapeDtypeStruct((t, B_COLS), F32),
        jax.ShapeDtypeStruct((t, SWA_WIDTH), BF16),
        jax.ShapeDtypeStruct((t, SWA_KV_WIDTH), BF16),
        jax.ShapeDtypeStruct((t, SWA_KV_WIDTH), BF16),
        jax.ShapeDtypeStruct((t, SWA_WIDTH), F32),
    ]
    out_specs = [
        row(MOBA_WIDTH), row(MOBA_WIDTH),
        pl.BlockSpec((nb_tile, 4, LANES + 16, MOBA_BLOCK), lambda i: (i, 0, 0, 0)),
        row(MOBA_WIDTH),
        pl.BlockSpec((nb_tile, 1, MOBA_WIDTH), lambda i: (i, 0, 0)),
        row(B_COLS), row(SWA_WIDTH), row(SWA_KV_WIDTH), row(SWA_KV_WIDTH), row(SWA_WIDTH),
    ]
    return pl.pallas_call(
        functools.partial(_proj_kernel, tiles_per_seq=seq // tm),
        grid=(t // tm,),
        in_specs=[row(d), row(LANES), row(LANES), full(wa), full(wb), full(wc), full(mu)],
        out_specs=out_specs,
        out_shape=out_shape,
        scratch_shapes=[pltpu.VMEM((1, B_COLS), F32)],
        compiler_params=pltpu.CompilerParams(dimension_semantics=("arbitrary",),
                                             vmem_limit_bytes=VMEM_LIMIT),
        name="proj",
    )(x2, cos, sin, wa, wb, wc, mu)


def _moba_kernel(q_ref, k_ref, vt_ref, kmean_ref, z_ref, o_ref, acc_ref):
    i = pl.program_id(2)
    nblk = kmean_ref.shape[0]
    bq = q_ref.shape[0]
    qt = q_ref[...].astype(F32).T
    row = lax.broadcasted_iota(jnp.int32, qt.shape, 0)
    qtm = jnp.concatenate([jnp.where(row < HEAD_DIM, qt, 0.0),
                           jnp.where(row >= HEAD_DIM, qt, 0.0)], axis=1).astype(BF16)

    gate = _dot(kmean_ref[...].astype(BF16), qtm)
    blk = lax.broadcasted_iota(jnp.int32, gate.shape, 0).astype(F32)
    past = blk < i.astype(F32)
    g = jnp.where(past, gate, -jnp.inf)
    bias = jnp.full(gate.shape, NEG, F32)
    for _ in range(MOBA_TOPK):
        mx = jnp.max(g, axis=0, keepdims=True)
        first = jnp.min(jnp.where(g == mx, blk, float(nblk)), axis=0, keepdims=True)
        hit = blk == first
        bias = jnp.where(hit, 0.0, bias)
        g = jnp.where(hit, -jnp.inf, g)
    bias = jnp.where(past, bias, NEG)
    qaug = jnp.concatenate([qtm, bias.astype(BF16),
                            jnp.zeros((LANES - nblk, 2 * bq), BF16)], axis=0)

    acc_ref[...] = jnp.zeros(acc_ref.shape, F32)
    lane = lax.broadcasted_iota(jnp.int32, (MOBA_BLOCK, LANES), 1)

    def update(st, n, m):
        mx = jnp.max(st, axis=0, keepdims=True)
        m_new = jnp.maximum(m, mx)
        alpha = jnp.exp(m - m_new)
        pt = jnp.exp(st - m_new).astype(BF16)
        acc_ref[...] = acc_ref[...] * alpha + _dot(vt_ref[n], pt)
        return m_new

    def past_block(n, m):
        kn = k_ref[pl.ds(pl.multiple_of(n * MOBA_BLOCK, MOBA_BLOCK), MOBA_BLOCK), :]
        onehot = jnp.where(lane == n, 1.0, 0.0).astype(BF16)
        st = _dot(jnp.concatenate([kn, onehot], axis=1), qaug)
        return update(st, n, m)

    m = lax.fori_loop(0, i, past_block, jnp.full((1, 2 * bq), NEG, F32))

    kn = k_ref[pl.ds(pl.multiple_of(i * MOBA_BLOCK, MOBA_BLOCK), MOBA_BLOCK), :]
    st = _dot(kn, qtm)
    kpos = lax.broadcasted_iota(jnp.int32, st.shape, 0)
    qpos = lax.broadcasted_iota(jnp.int32, st.shape, 1) & (bq - 1)
    update(jnp.where(kpos <= qpos, st, NEG), i, m)

    acc = acc_ref[...]
    ot = acc[0:LANES, :] / acc[LANES:LANES + 1, :]
    o = ot.T
    lane_o = lax.broadcasted_iota(jnp.int32, (bq, LANES), 1)
    o2 = jnp.where(lane_o < HEAD_DIM, o[0:bq], o[bq:2 * bq])
    o_ref[...] = (o2 * _silu(z_ref[...])).astype(BF16)


def _moba(qa, ka, vta, kmean, za):
    b, s, _ = qa.shape
    nblk = s // MOBA_BLOCK
    assert nblk <= LANES and MOBA_BLOCK & (MOBA_BLOCK - 1) == 0
    npair = MOBA_WIDTH // LANES
    tile = pl.BlockSpec((None, MOBA_BLOCK, LANES), lambda bi, p, i: (bi, i, p))
    return pl.pallas_call(
        _moba_kernel,
        grid=(b, npair, nblk),
        in_specs=[tile,
                  pl.BlockSpec((None, s, LANES), lambda bi, p, i: (bi, 0, p)),
                  pl.BlockSpec((None, nblk, None, LANES + 16, MOBA_BLOCK), lambda bi, p, i: (bi, 0, p, 0, 0)),
                  pl.BlockSpec((None, nblk, LANES), lambda bi, p, i: (bi, 0, p)),
                  tile],
        out_specs=tile,
        out_shape=jax.ShapeDtypeStruct((b, s, MOBA_WIDTH), BF16),
        scratch_shapes=[pltpu.VMEM((LANES + 16, 2 * MOBA_BLOCK), F32)],
        compiler_params=pltpu.CompilerParams(dimension_semantics=("arbitrary",) * 3,
                                             vmem_limit_bytes=VMEM_LIMIT),
        name="moba",
    )(qa, ka, vta, kmean, za)


def _segsum(xs, ones256):
    m = xs[0].shape[0]
    parts = []
    for x in xs:
        for piece in _split3(x):
            parts += [piece[:, :256], piece[:, 256:]]
    res = _dot(jnp.concatenate(parts, axis=0), ones256)
    outs = []
    for idx in range(len(xs)):
        r = [res[(idx * 6 + j) * m:(idx * 6 + j + 1) * m] for j in range(6)]
        outs.append(jnp.concatenate([r[0] + r[2] + r[4], r[1] + r[3] + r[5]], axis=1))
    return outs


def _rwkv_pair(at, rt, kh, bh, kt, bt, v, ptot, s2, masks):
    lane_lo, strict, incl, bdiag = masks
    c = at.shape[0]
    at0 = jnp.where(lane_lo, at, 0.0)
    at1 = jnp.where(lane_lo, 0.0, at)
    xw = jnp.concatenate([at0, at1], axis=0)
    lhs = jnp.concatenate([xw, jnp.where(lane_lo, rt, 0.0), jnp.where(lane_lo, 0.0, rt)], axis=0)
    rhs = jnp.concatenate([bh, kh], axis=0)
    sc = _dot_nt(lhs.astype(BF16), rhs.astype(BF16))
    sca = jnp.where(strict, sc[0:2 * c], 0.0)
    scr = jnp.where(incl, sc[2 * c:4 * c], 0.0)
    sca = jnp.concatenate([sca[0:c], pltpu.roll(sca[c:2 * c], c, 1)], axis=0)
    n2 = jnp.where(bdiag, sca, 0.0)
    mak = jnp.where(bdiag, 0.0, sca)
    z = jnp.concatenate([jnp.where(lane_lo, 0.0, v), jnp.where(lane_lo, v, 0.0)], axis=0)
    xu = _dot(mak.astype(BF16), z.astype(BF16))
    x = jnp.concatenate([xw, xu], axis=1)
    p = n2
    steps = (c - 1).bit_length()
    for j in range(steps):
        pb = p.astype(BF16)
        x = x + _dot(pb, x.astype(BF16))
        if j + 1 < steps:
            p = _dot(pb, pb)
    wp = x[0:c, 0:LANES] + x[c:2 * c, 0:LANES]
    u0 = x[0:c, LANES:] + x[c:2 * c, LANES:]
    s2b = s2.astype(BF16)
    wr = _dot_nt(jnp.concatenate([wp, rt], axis=0).astype(BF16), s2b)
    u = wr[0:c] + u0
    vu = jnp.concatenate([u, v], axis=0).astype(BF16)
    ym = _dot(scr.astype(BF16), vu)
    y = wr[c:2 * c] + jnp.where(lane_lo, ym[0:c], ym[c:2 * c])
    bk = jnp.concatenate([bt, kt], axis=0).astype(BF16)
    s2_new = s2 * ptot + jnp.where(bdiag, _dot_tn(vu, bk), 0.0)
    return y, s2_new


def _rwkv_kernel(h_ref, vec_ref, wcomb_ref, ones_ref, o_ref, st_ref):
    ci = pl.program_id(0)
    nb, c, _ = h_ref.shape
    m = nb * c
    w = RWKV_WIDTH

    @pl.when(ci == 0)
    def _():
        st_ref[...] = jnp.zeros(st_ref.shape, F32)

    h = h_ref[...].reshape(m, h_ref.shape[2])
    r = h[:, 0:w]
    k = h[:, w:2 * w]
    v = h[:, 2 * w:3 * w]
    lo = h[:, 3 * w:3 * w + 2 * LORA]
    z = h[:, 3 * w + 2 * LORA:]
    w0, a0, k_k, k_a, r_k, gn_g, gn_b = [vec_ref[j:j + 1, :] for j in range(7)]
    ones256 = ones_ref[...]

    lane_m = lax.broadcasted_iota(jnp.int32, (m, LANES), 1)
    lo = jnp.where(lane_m < LORA, jnp.tanh(lo), lo)
    wa = _dot(lo.astype(BF16), wcomb_ref[...])
    logw = -DECAY_SCALE * jax.nn.sigmoid(w0 + wa[:, 0:w])
    a = jax.nn.sigmoid(a0 + wa[:, w:])
    kkr = k * k_k
    k2 = k * (1.0 + (a - 1.0) * k_a)
    ss, rk = _segsum([kkr * kkr, r * k2 * r_k], ones256)
    kk = kkr / jnp.maximum(jnp.sqrt(ss), 1e-12)
    aa = -kk
    bb = kk * a

    tr = lax.broadcasted_iota(jnp.int32, (m, m), 0)
    tc = lax.broadcasted_iota(jnp.int32, (m, m), 1)
    sh = c.bit_length() - 1
    tri = jnp.where(tr >= tc, jnp.where((tr >> sh) == (tc >> sh), 1.0, 0.0), 0.0).astype(BF16)
    cinc = sum(_dot(tri, piece) for piece in _split3(logw))
    ctot = jnp.concatenate([jnp.broadcast_to(cinc[(b + 1) * c - 1:(b + 1) * c], (c, w)) for b in range(nb)], axis=0)
    einv = jnp.exp(-cinc)
    edec = jnp.exp(ctot - cinc)
    at = aa * jnp.exp(cinc - logw)
    rt = r * jnp.exp(cinc)
    kh = k2 * einv
    bh = bb * einv
    kt = k2 * edec
    bt = bb * edec
    ptot = jnp.exp(ctot)

    r2 = lax.broadcasted_iota(jnp.int32, (2 * c, 2 * c), 0)
    c2 = lax.broadcasted_iota(jnp.int32, (2 * c, 2 * c), 1)
    ri = r2 & (c - 1)
    cj = c2 & (c - 1)
    masks = (lax.broadcasted_iota(jnp.int32, (c, LANES), 1) < HEAD_DIM,
             ri > cj, ri >= cj, (r2 >> sh) == (c2 >> sh))

    ys = []
    for b in range(nb):
        rows = slice(b * c, (b + 1) * c)
        yb = []
        for p in range(w // LANES):
            cols = slice(p * LANES, (p + 1) * LANES)
            y, s_new = _rwkv_pair(at[rows, cols], rt[rows, cols], kh[rows, cols], bh[rows, cols],
                                  kt[rows, cols], bt[rows, cols], v[rows, cols],
                                  ptot[b * c:b * c + 1, cols], st_ref[b, p], masks)
            st_ref[b, p] = s_new
            yb.append(y)
        ys.append(jnp.concatenate(yb, axis=1))
    y = jnp.concatenate(ys, axis=0)

    (ysum,) = _segsum([y], ones256)
    d = y - ysum * (1.0 / HEAD_DIM)
    (dsq,) = _segsum([d * d], ones256)
    yn = d * lax.rsqrt(dsq * (1.0 / HEAD_DIM) + GN_EPS) * gn_g + gn_b
    ob = yn + rk * v
    ub = (ob * _silu(z)).astype(BF16)
    for b in range(nb):
        o_ref[b] = ub[b * c:(b + 1) * c]


def _rwkv(hb, vecs, wcomb, ones256):
    b, s, wcols = hb.shape
    c = RWKV_CHUNK
    assert c & (c - 1) == 0 and 2 * c == LANES
    full = lambda a: pl.BlockSpec(a.shape, lambda i: (0,) * a.ndim)
    return pl.pallas_call(
        _rwkv_kernel,
        grid=(s // c,),
        in_specs=[pl.BlockSpec((b, c, wcols), lambda i: (0, i, 0)), full(vecs), full(wcomb), full(ones256)],
        out_specs=pl.BlockSpec((b, c, RWKV_WIDTH), lambda i: (0, i, 0)),
        out_shape=jax.ShapeDtypeStruct((b, s, RWKV_WIDTH), BF16),
        scratch_shapes=[pltpu.VMEM((b, RWKV_WIDTH // LANES, LANES, LANES), F32)],
        compiler_params=pltpu.CompilerParams(dimension_semantics=("arbitrary",),
                                             vmem_limit_bytes=VMEM_LIMIT),
        name="rwkv",
    )(hb, vecs, wcomb, ones256)


def _swa_kernel(sink_ref, q_ref, kp_ref, kc_ref, vp_ref, vc_ref, z_ref, o_ref):
    n = pl.program_id(1)
    bq = q_ref.shape[0]
    q = q_ref[...].astype(F32)
    kband = jnp.concatenate([kp_ref[...], kc_ref[...]], axis=0).astype(F32)
    vband = jnp.concatenate([vp_ref[...], vc_ref[...]], axis=0).astype(F32)
    rows = SWA_GROUP * bq
    r = lax.broadcasted_iota(jnp.int32, (rows, 2 * bq), 0)
    cc = lax.broadcasted_iota(jnp.int32, (rows, 2 * bq), 1)
    d = cc - (r & (bq - 1))
    lo_ok = jnp.where(n > 0, 0, bq)
    d = jnp.where(cc >= lo_ok, d, 0)
    r1 = lax.broadcasted_iota(jnp.int32, (rows, 1), 0)
    outs = []
    for g in range(SWA_KV_WIDTH // HEAD_DIM):
        kg = kband[:, g * HEAD_DIM:(g + 1) * HEAD_DIM].astype(BF16)
        vg = vband[:, g * HEAD_DIM:(g + 1) * HEAD_DIM].astype(BF16)
        heads = range(g * SWA_GROUP, (g + 1) * SWA_GROUP)
        qg = jnp.concatenate([q[:, j * HEAD_DIM:(j + 1) * HEAD_DIM] for j in heads], axis=0).astype(BF16)
        s = _dot_nt(qg, kg)
        s = jnp.where(d >= 1, jnp.where(d <= bq, s, NEG), NEG)
        sink = jnp.zeros((rows, 1), F32)
        for jj, j in enumerate(heads):
            sink = jnp.where(r1 >= jj * bq, sink_ref[j], sink)
        mx = jnp.maximum(jnp.max(s, axis=1, keepdims=True), sink)
        p = jnp.exp(s - mx)
        denom = jnp.sum(p, axis=1, keepdims=True) + jnp.exp(sink - mx)
        outs.append(_dot((p / denom).astype(BF16), vg))
    oc = jnp.concatenate([outs[g][jj * bq:(jj + 1) * bq]
                          for g in range(len(outs)) for jj in range(SWA_GROUP)], axis=1)
    o_ref[...] = (oc * _silu(z_ref[...])).astype(BF16)


def _swa(sinks, qc, kc, vc, zc):
    b, s, _ = qc.shape
    bq = SWA_BLOCK
    assert bq & (bq - 1) == 0
    cur = lambda w: pl.BlockSpec((None, bq, w), lambda bi, n, sk: (bi, n, 0))
    prv = lambda w: pl.BlockSpec((None, bq, w), lambda bi, n, sk: (bi, jnp.maximum(n - 1, 0), 0))
    return pl.pallas_call(
        _swa_kernel,
        grid_spec=pltpu.PrefetchScalarGridSpec(
            num_scalar_prefetch=1,
            grid=(b, s // bq),
            in_specs=[cur(SWA_WIDTH), prv(SWA_KV_WIDTH), cur(SWA_KV_WIDTH),
                      prv(SWA_KV_WIDTH), cur(SWA_KV_WIDTH), cur(SWA_WIDTH)],
            out_specs=cur(SWA_WIDTH)),
        out_shape=jax.ShapeDtypeStruct((b, s, SWA_WIDTH), BF16),
        compiler_params=pltpu.CompilerParams(dimension_semantics=("arbitrary",) * 2,
                                             vmem_limit_bytes=VMEM_LIMIT),
        name="swa",
    )(sinks, qc, kc, kc, vc, vc, zc)


def _out_kernel(x_ref, ua_ref, ub_ref, uc_ref, wg_ref, wa_ref, wb_ref, wc_ref, wo_ref, lng_ref, lnb_ref,
                o_ref, *, alpha):
    x = x_ref[...]
    d = x.shape[1]
    g = _dot(x.astype(BF16), wg_ref[...])
    merged = (jax.nn.sigmoid(g[:, 0:d]) * _dot(ua_ref[...], wa_ref[...])
              + jax.nn.sigmoid(g[:, d:2 * d]) * _dot(ub_ref[...], wb_ref[...])
              + jax.nn.sigmoid(g[:, 2 * d:]) * _dot(uc_ref[...], wc_ref[...]))
    y = alpha * x + _dot(merged.astype(BF16), wo_ref[...])
    mu = jnp.mean(y, axis=-1, keepdims=True)
    dev = y - mu
    var = jnp.mean(dev * dev, axis=-1, keepdims=True)
    o_ref[...] = dev * lax.rsqrt(var + LN_EPS) * lng_ref[...] + lnb_ref[...]


def _out(x2, ua, ub, uc, wg, wa, wb, wc, wo, lng, lnb, alpha, tm=256):
    t, d = x2.shape
    row = lambda w: pl.BlockSpec((tm, w), lambda i: (i, 0))
    full = lambda a: pl.BlockSpec(a.shape, lambda i: (0,) * a.ndim)
    return pl.pallas_call(
        functools.partial(_out_kernel, alpha=alpha),
        grid=(t // tm,),
        in_specs=[row(d), row(ua.shape[1]), row(ub.shape[1]), row(uc.shape[1]),
                  full(wg), full(wa), full(wb), full(wc), full(wo), full(lng), full(lnb)],
        out_specs=row(d),
        out_shape=jax.ShapeDtypeStruct((t, d), F32),
        compiler_params=pltpu.CompilerParams(dimension_semantics=("arbitrary",),
                                             vmem_limit_bytes=VMEM_LIMIT),
        name="merge_out",
    )(x2, ua, ub, uc, wg, wa, wb, wc, wo, lng, lnb)


def kernel(x, positions, w_in, mu_rwkv, w0, w_up, a0, a_up, k_k, k_a, r_k, gn_g, gn_b, sinks,
           w_branch_a, w_branch_b, w_branch_c, w_out, ln_g, ln_b):
    b, s, d = x.shape
    depth = w_in.shape[0]
    t = b * s
    alpha = (2 * depth) ** 0.25
    assert w_in.shape[2] == A_COLS + B_COLS + C_COLS + 3 * d

    half = HEAD_DIM // 2
    inv = jnp.power(ROPE_THETA, -jnp.arange(half, dtype=F32) / half)
    inv_row = jnp.tile(inv, LANES // half).reshape(1, LANES)
    cos, sin = _rope_tables(positions.reshape(t, 1), inv_row)

    head_of_lane = jnp.arange(256) // HEAD_DIM
    ones256 = (head_of_lane[:, None] == head_of_lane[None, :]).astype(BF16)

    x2 = x.reshape(t, d)
    for l in range(depth):
        wl = w_in[l].astype(BF16)
        wa = wl[:, 0:A_COLS]
        wb = wl[:, A_COLS:A_COLS + B_COLS]
        wc = wl[:, A_COLS + B_COLS:A_COLS + B_COLS + C_COLS]
        wg = wl[:, A_COLS + B_COLS + C_COLS:]
        qa, ka, vta, za, kmean, hb, qc, kc, vc, zc = _proj(
            x2, cos, sin, wa, wb, wc, mu_rwkv[l].reshape(1, B_COLS), s)

        nblk = s // MOBA_BLOCK
        ua = _moba(qa.reshape(b, s, MOBA_WIDTH), ka.reshape(b, s, MOBA_WIDTH),
                   vta.reshape(b, nblk, 4, LANES + 16, MOBA_BLOCK),
                   kmean.reshape(b, nblk, MOBA_WIDTH), za.reshape(b, s, MOBA_WIDTH))

        zeros_lora = jnp.zeros((LORA, RWKV_WIDTH), F32)
        wcomb = jnp.concatenate([jnp.concatenate([w_up[l], zeros_lora], axis=1),
                                 jnp.concatenate([zeros_lora, a_up[l]], axis=1)], axis=0).astype(BF16)
        vecs = jnp.stack([w0[l], a0[l], k_k[l], k_a[l], r_k[l].reshape(-1), gn_g[l], gn_b[l],
                          jnp.zeros((RWKV_WIDTH,), F32)], axis=0)
        ub = _rwkv(hb.reshape(b, s, B_COLS), vecs, wcomb, ones256)

        uc = _swa(sinks[l], qc.reshape(b, s, SWA_WIDTH), kc.reshape(b, s, SWA_KV_WIDTH),
                  vc.reshape(b, s, SWA_KV_WIDTH), zc.reshape(b, s, SWA_WIDTH))

        x2 = _out(x2, ua.reshape(t, MOBA_WIDTH), ub.reshape(t, RWKV_WIDTH), uc.reshape(t, SWA_WIDTH),
                  wg, w_branch_a[l].astype(BF16), w_branch_b[l].astype(BF16), w_branch_c[l].astype(BF16),
                  w_out[l].astype(BF16), ln_g[l].reshape(1, d), ln_b[l].reshape(1, d), alpha)
    return x2.reshape(b, s, d)
```

```python
import functools
import math

import jax
import jax.numpy as jnp
from jax import lax
from jax.experimental import pallas as pl
from jax.experimental.pallas import tpu as pltpu

F32 = jnp.float32
BF16 = jnp.bfloat16

HEAD_DIM = 64
LANES = 128
MOBA_BLOCK = 256
MOBA_TOPK = 3
MOBA_WIDTH = 512
RWKV_WIDTH = 512
LORA = 64
RWKV_CHUNK = 64
SWA_WIDTH = 512
SWA_KV_WIDTH = 128
SWA_GROUP = 4
SWA_BLOCK = 128
ROPE_THETA = 10000.0
LN_EPS = 1e-5
GN_EPS = HEAD_DIM * 1e-5
DECAY_SCALE = math.exp(-0.5)
NEG = -1e30
VMEM_LIMIT = 56 * 1024 * 1024

A_COLS = 4 * MOBA_WIDTH
B_COLS = 3 * RWKV_WIDTH + 2 * LORA + RWKV_WIDTH
C_COLS = SWA_WIDTH + 2 * SWA_KV_WIDTH + SWA_WIDTH


def _dot(a, b):
    return jnp.dot(a, b, preferred_element_type=F32)


def _dot_nt(a, b):
    return lax.dot_general(a, b, (((1,), (1,)), ((), ())), preferred_element_type=F32)


def _dot_tn(a, b):
    return lax.dot_general(a, b, (((0,), (0,)), ((), ())), preferred_element_type=F32)


def _silu(z):
    return z * jax.nn.sigmoid(z)


def _split3(x):
    hi = x.astype(BF16)
    r1 = x - hi.astype(F32)
    mid = r1.astype(BF16)
    lo = (r1 - mid.astype(F32)).astype(BF16)
    return hi, mid, lo


def _rope_table_kernel(pos_ref, inv_ref, cos_ref, sin_ref):
    ang = pos_ref[...].astype(F32) * inv_ref[...]
    lane = lax.broadcasted_iota(jnp.int32, ang.shape, 1)
    s = jnp.sin(ang)
    cos_ref[...] = jnp.cos(ang)
    sin_ref[...] = jnp.where((lane & 32) == 0, -s, s)


def _rope_tables(pos_col, inv_row, tm=1024):
    t = pos_col.shape[0]
    return pl.pallas_call(
        _rope_table_kernel,
        grid=(t // tm,),
        in_specs=[pl.BlockSpec((tm, 1), lambda i: (i, 0)),
                  pl.BlockSpec((1, LANES), lambda i: (0, 0))],
        out_specs=[pl.BlockSpec((tm, LANES), lambda i: (i, 0))] * 2,
        out_shape=[jax.ShapeDtypeStruct((t, LANES), F32)] * 2,
        name="rope_table",
    )(pos_col, inv_row)


def _proj_kernel(x_ref, cos_ref, sin_ref, wa_ref, wb_ref, wc_ref, mu_ref,
                 qa_ref, ka_ref, vta_ref, za_ref, kmean_ref, hb_ref,
                 qc_ref, kc_ref, vc_ref, zc_ref, carry_ref, *, tiles_per_seq):
    i = pl.program_id(0)
    xb = x_ref[...].astype(BF16)
    tm = xb.shape[0]
    cos = cos_ref[...]
    sin = sin_ref[...]
    lane = lax.broadcasted_iota(jnp.int32, (tm, LANES), 1)
    first_half = (lane & 32) == 0

    def rope(blk):
        partner = jnp.where(first_half, pltpu.roll(blk, 96, 1), pltpu.roll(blk, 32, 1))
        return blk * cos + partner * sin

    scale = HEAD_DIM ** -0.5
    nblk = tm // MOBA_BLOCK

    ha = _dot(xb, wa_ref[...])
    for j in range(MOBA_WIDTH // LANES):
        sl = slice(j * LANES, (j + 1) * LANES)
        qa_ref[:, sl] = (rope(ha[:, sl]) * scale).astype(BF16)
        kj = rope(ha[:, MOBA_WIDTH + j * LANES:MOBA_WIDTH + (j + 1) * LANES])
        ka_ref[:, sl] = kj.astype(BF16)
        for r in range(nblk):
            kmean_ref[r, :, sl] = jnp.mean(kj[r * MOBA_BLOCK:(r + 1) * MOBA_BLOCK], axis=0, keepdims=True)
    va = ha[:, 2 * MOBA_WIDTH:3 * MOBA_WIDTH]
    ones_rows = jnp.ones((16, MOBA_BLOCK), BF16)
    for r in range(nblk):
        vt = va[r * MOBA_BLOCK:(r + 1) * MOBA_BLOCK].T
        for p in range(MOBA_WIDTH // LANES):
            vta_ref[r, p, 0:LANES, :] = vt[p * LANES:(p + 1) * LANES].astype(BF16)
            vta_ref[r, p, LANES:LANES + 16, :] = ones_rows
    za_ref[...] = ha[:, 3 * MOBA_WIDTH:]

    hb = _dot(xb, wb_ref[...])
    prev = pltpu.roll(hb, 1, 0)
    row = lax.broadcasted_iota(jnp.int32, hb.shape, 0)

    @pl.when(i % tiles_per_seq == 0)
    def _():
        carry_ref[...] = jnp.zeros(carry_ref.shape, F32)

    prev = jnp.where(row == 0, carry_ref[...], prev)
    hb_ref[...] = hb + (prev - hb) * mu_ref[...]
    carry_ref[...] = hb[tm - 1:tm, :]

    hc = _dot(xb, wc_ref[...])
    for j in range(SWA_WIDTH // LANES):
        sl = slice(j * LANES, (j + 1) * LANES)
        qc_ref[:, sl] = (rope(hc[:, sl]) * scale).astype(BF16)
    kc_ref[...] = rope(hc[:, SWA_WIDTH:SWA_WIDTH + SWA_KV_WIDTH]).astype(BF16)
    vc_ref[...] = hc[:, SWA_WIDTH + SWA_KV_WIDTH:SWA_WIDTH + 2 * SWA_KV_WIDTH].astype(BF16)
    zc_ref[...] = hc[:, SWA_WIDTH + 2 * SWA_KV_WIDTH:]


def _proj(x2, cos, sin, wa, wb, wc, mu, seq, tm=256):
    t, d = x2.shape
    nb_tile = tm // MOBA_BLOCK
    row = lambda w: pl.BlockSpec((tm, w), lambda i: (i, 0))
    full = lambda a: pl.BlockSpec(a.shape, lambda i: (0,) * a.ndim)
    out_shape = [
        jax.ShapeDtypeStruct((t, MOBA_WIDTH), BF16),
        jax.ShapeDtypeStruct((t, MOBA_WIDTH), BF16),
        jax.ShapeDtypeStruct((t // MOBA_BLOCK, 4, LANES + 16, MOBA_BLOCK), BF16),
        jax.ShapeDtypeStruct((t, MOBA_WIDTH), F32),
        jax.ShapeDtypeStruct((t // MOBA_BLOCK, 1, MOBA_WIDTH), F32),
        jax.ShapeDtypeStruct((t, B_COLS), F32),
        jax.ShapeDtypeStruct((t, SWA_WIDTH), BF16),
        jax.ShapeDtypeStruct((t, SWA_KV_WIDTH), BF16),
        jax.ShapeDtypeStruct((t, SWA_KV_WIDTH), BF16),
        jax.ShapeDtypeStruct((t, SWA_WIDTH), F32),
    ]
    out_specs = [
        row(MOBA_WIDTH), row(MOBA_WIDTH),
        pl.BlockSpec((nb_tile, 4, LANES + 16, MOBA_BLOCK), lambda i: (i, 0, 0, 0)),
        row(MOBA_WIDTH),
        pl.BlockSpec((nb_tile, 1, MOBA_WIDTH), lambda i: (i, 0, 0)),
        row(B_COLS), row(SWA_WIDTH), row(SWA_KV_WIDTH), row(SWA_KV_WIDTH), row(SWA_WIDTH),
    ]
    return pl.pallas_call(
        functools.partial(_proj_kernel, tiles_per_seq=seq // tm),
        grid=(t // tm,),
        in_specs=[row(d), row(LANES), row(LANES), full(wa), full(wb), full(wc), full(mu)],
        out_specs=out_specs,
        out_shape=out_shape,
        scratch_shapes=[pltpu.VMEM((1, B_COLS), F32)],
        compiler_params=pltpu.CompilerParams(dimension_semantics=("arbitrary",),
                                             vmem_limit_bytes=VMEM_LIMIT),
        name="proj",
    )(x2, cos, sin, wa, wb, wc, mu)


def _moba_kernel(q_ref, k_ref, vt_ref, kmean_ref, z_ref, o_ref, acc_ref):
    i = pl.program_id(1)
    nb, nblk, _ = kmean_ref.shape
    bq = q_ref.shape[1]
    bs = range(nb)
    row = lax.broadcasted_iota(jnp.int32, (LANES, bq), 0)
    blk = lax.broadcasted_iota(jnp.int32, (nblk, 2 * bq), 0).astype(F32)
    i_f = i.astype(F32)
    past = blk < i_f
    qaug = []
    for b in bs:
        qt = q_ref[b].astype(F32).T
        qtm = jnp.concatenate([jnp.where(row < HEAD_DIM, qt, 0.0),
                               jnp.where(row >= HEAD_DIM, qt, 0.0)], axis=1).astype(BF16)
        gate = _dot(kmean_ref[b].astype(BF16), qtm)
        g = jnp.where(past, gate, -jnp.inf)
        bias = jnp.full(gate.shape, NEG, F32)
        for _ in range(MOBA_TOPK):
            mx = jnp.max(g, axis=0, keepdims=True)
            first = jnp.min(jnp.where(g == mx, blk, float(nblk)), axis=0, keepdims=True)
            hit = blk == first
            bias = jnp.where(hit, 0.0, bias)
            g = jnp.where(hit, -jnp.inf, g)
        bias = jnp.where(past, bias, jnp.where(blk == i_f, 0.0, NEG))
        qaug.append(jnp.concatenate([qtm, bias.astype(BF16),
                                     jnp.zeros((LANES - nblk, 2 * bq), BF16)], axis=0))

    acc_ref[...] = jnp.zeros(acc_ref.shape, F32)
    lane = lax.broadcasted_iota(jnp.int32, (MOBA_BLOCK, LANES), 1)

    def scores(n):
        onehot = jnp.where(lane == n, 1.0, 0.0).astype(BF16)
        off = pl.multiple_of(n * MOBA_BLOCK, MOBA_BLOCK)
        return [_dot(jnp.concatenate([k_ref[b, pl.ds(off, MOBA_BLOCK), :], onehot], axis=1), qaug[b]) for b in bs]

    def update(st, n, m):
        m_new = [jnp.maximum(m[b], jnp.max(st[b], axis=0, keepdims=True)) for b in bs]
        alpha = [jnp.exp(m[b] - m_new[b]) for b in bs]
        pt = [jnp.exp(st[b] - m_new[b]).astype(BF16) for b in bs]
        pv = [_dot(vt_ref[b, n], pt[b]) for b in bs]
        for b in bs:
            acc_ref[b] = acc_ref[b] * alpha[b] + pv[b]
        return m_new

    def past_block(n, carry):
        m, st = carry
        st_next = scores(n + 1)
        return update(st, n, m), st_next

    m, st = lax.fori_loop(0, i, past_block, ([jnp.full((1, 2 * bq), NEG, F32) for _ in bs], scores(0)))

    kpos = lax.broadcasted_iota(jnp.int32, st[0].shape, 0)
    qpos = lax.broadcasted_iota(jnp.int32, st[0].shape, 1) & (bq - 1)
    update([jnp.where(kpos <= qpos, st[b], NEG) for b in bs], i, m)

    lane_o = lax.broadcasted_iota(jnp.int32, (bq, LANES), 1)
    for b in bs:
        acc = acc_ref[b]
        ot = acc[0:LANES, :] / acc[LANES:LANES + 1, :]
        o = ot.T
        o2 = jnp.where(lane_o < HEAD_DIM, o[0:bq], o[bq:2 * bq])
        o_ref[b] = (o2 * _silu(z_ref[b])).astype(BF16)


def _moba(qa, ka, vta, kmean, za):
    b, s, _ = qa.shape
    nblk = s // MOBA_BLOCK
    assert nblk <= LANES and MOBA_BLOCK & (MOBA_BLOCK - 1) == 0
    npair = MOBA_WIDTH // LANES
    tile = pl.BlockSpec((b, MOBA_BLOCK, LANES), lambda p, i: (0, i, p))
    return pl.pallas_call(
        _moba_kernel,
        grid=(npair, nblk),
        in_specs=[tile,
                  pl.BlockSpec((b, s, LANES), lambda p, i: (0, 0, p)),
                  pl.BlockSpec((b, nblk, None, LANES + 16, MOBA_BLOCK), lambda p, i: (0, 0, p, 0, 0)),
                  pl.BlockSpec((b, nblk, LANES), lambda p, i: (0, 0, p)),
                  tile],
        out_specs=tile,
        out_shape=jax.ShapeDtypeStruct((b, s, MOBA_WIDTH), BF16),
        scratch_shapes=[pltpu.VMEM((b, LANES + 16, 2 * MOBA_BLOCK), F32)],
        compiler_params=pltpu.CompilerParams(dimension_semantics=("arbitrary",) * 2,
                                             vmem_limit_bytes=VMEM_LIMIT),
        name="moba",
    )(qa, ka, vta, kmean, za)


def _segsum(xs, ones256):
    m = xs[0].shape[0]
    parts = []
    for x in xs:
        for piece in _split3(x):
            parts += [piece[:, :256], piece[:, 256:]]
    res = _dot(jnp.concatenate(parts, axis=0), ones256)
    outs = []
    for idx in range(len(xs)):
        r = [res[(idx * 6 + j) * m:(idx * 6 + j + 1) * m] for j in range(6)]
        outs.append(jnp.concatenate([r[0] + r[2] + r[4], r[1] + r[3] + r[5]], axis=1))
    return outs


def _rwkv_pairs(at, rt, kh, bh, kt, bt, v, ptot, s2, masks):
    lane_lo, strict, incl, bdiag = masks
    c = at[0].shape[0]
    idx = range(len(at))
    lo = lambda t: jnp.where(lane_lo, t, 0.0)
    hi = lambda t: jnp.where(lane_lo, 0.0, t)
    xw = [jnp.concatenate([lo(at[i]), hi(at[i])], axis=0) for i in idx]
    lhs = [jnp.concatenate([xw[i], lo(rt[i]), hi(rt[i])], axis=0).astype(BF16) for i in idx]
    rhs = [jnp.concatenate([bh[i], kh[i]], axis=0).astype(BF16) for i in idx]
    sc = [_dot_nt(lhs[i], rhs[i]) for i in idx]
    scr = [jnp.where(incl, sc[i][2 * c:4 * c], 0.0).astype(BF16) for i in idx]
    sca = [jnp.where(strict, sc[i][0:2 * c], 0.0) for i in idx]
    sca = [jnp.concatenate([sca[i][0:c], pltpu.roll(sca[i][c:2 * c], c, 1)], axis=0) for i in idx]
    p = [jnp.where(bdiag, sca[i], 0.0).astype(BF16) for i in idx]
    mak = [jnp.where(bdiag, 0.0, sca[i]).astype(BF16) for i in idx]
    z = [jnp.concatenate([hi(v[i]), lo(v[i])], axis=0).astype(BF16) for i in idx]
    x = [jnp.concatenate([xw[i], _dot(mak[i], z[i])], axis=1) for i in idx]
    steps = (c - 1).bit_length()
    for j in range(steps):
        x = [x[i] + _dot(p[i], x[i].astype(BF16)) for i in idx]
        if j + 1 < steps:
            p = [_dot(p[i], p[i]).astype(BF16) for i in idx]
    wp = [x[i][0:c, 0:LANES] + x[i][c:2 * c, 0:LANES] for i in idx]
    u0 = [x[i][0:c, LANES:] + x[i][c:2 * c, LANES:] for i in idx]
    wr = [_dot_nt(jnp.concatenate([wp[i], rt[i]], axis=0).astype(BF16), s2[i].astype(BF16)) for i in idx]
    vu = [jnp.concatenate([wr[i][0:c] + u0[i], v[i]], axis=0).astype(BF16) for i in idx]
    ym = [_dot(scr[i], vu[i]) for i in idx]
    y = [wr[i][c:2 * c] + jnp.where(lane_lo, ym[i][0:c], ym[i][c:2 * c]) for i in idx]
    bk = [jnp.concatenate([bt[i], kt[i]], axis=0).astype(BF16) for i in idx]
    s2_new = [s2[i] * ptot[i] + jnp.where(bdiag, _dot_tn(vu[i], bk[i]), 0.0) for i in idx]
    return y, s2_new


def _rwkv_kernel(h_ref, vec_ref, wcomb_ref, ones_ref, o_ref, st_ref):
    ci = pl.program_id(0)
    nb, c, _ = h_ref.shape
    m = nb * c
    w = RWKV_WIDTH

    @pl.when(ci == 0)
    def _():
        st_ref[...] = jnp.zeros(st_ref.shape, F32)

    h = h_ref[...].reshape(m, h_ref.shape[2])
    r = h[:, 0:w]
    k = h[:, w:2 * w]
    v = h[:, 2 * w:3 * w]
    lo = h[:, 3 * w:3 * w + 2 * LORA]
    z = h[:, 3 * w + 2 * LORA:]
    w0, a0, k_k, k_a, r_k, gn_g, gn_b = [vec_ref[j:j + 1, :] for j in range(7)]
    ones256 = ones_ref[...]

    lane_m = lax.broadcasted_iota(jnp.int32, (m, LANES), 1)
    lo = jnp.where(lane_m < LORA, jnp.tanh(lo), lo)
    wa = _dot(lo.astype(BF16), wcomb_ref[...])
    logw = -DECAY_SCALE * jax.nn.sigmoid(w0 + wa[:, 0:w])
    a = jax.nn.sigmoid(a0 + wa[:, w:])
    kkr = k * k_k
    k2 = k * (1.0 + (a - 1.0) * k_a)
    ss, rk = _segsum([kkr * kkr, r * k2 * r_k], ones256)
    kk = kkr / jnp.maximum(jnp.sqrt(ss), 1e-12)
    aa = -kk
    bb = kk * a

    tr = lax.broadcasted_iota(jnp.int32, (m, m), 0)
    tc = lax.broadcasted_iota(jnp.int32, (m, m), 1)
    sh = c.bit_length() - 1
    tri = jnp.where(tr >= tc, jnp.where((tr >> sh) == (tc >> sh), 1.0, 0.0), 0.0).astype(BF16)
    cinc = sum(_dot(tri, piece) for piece in _split3(logw))
    ctot = jnp.concatenate([jnp.broadcast_to(cinc[(b + 1) * c - 1:(b + 1) * c], (c, w)) for b in range(nb)], axis=0)
    einv = jnp.exp(-cinc)
    edec = jnp.exp(ctot - cinc)
    at = aa * jnp.exp(cinc - logw)
    rt = r * jnp.exp(cinc)
    kh = k2 * einv
    bh = bb * einv
    kt = k2 * edec
    bt = bb * edec
    ptot = jnp.exp(ctot)

    r2 = lax.broadcasted_iota(jnp.int32, (2 * c, 2 * c), 0)
    c2 = lax.broadcasted_iota(jnp.int32, (2 * c, 2 * c), 1)
    ri = r2 & (c - 1)
    cj = c2 & (c - 1)
    masks = (lax.broadcasted_iota(jnp.int32, (c, LANES), 1) < HEAD_DIM,
             ri > cj, ri >= cj, (r2 >> sh) == (c2 >> sh))

    npair = w // LANES
    pairs = [(b, p) for b in range(nb) for p in range(npair)]
    cut = lambda t: [t[b * c:(b + 1) * c, p * LANES:(p + 1) * LANES] for b, p in pairs]
    ypairs, s_new = _rwkv_pairs(cut(at), cut(rt), cut(kh), cut(bh), cut(kt), cut(bt), cut(v),
                                [ptot[b * c:b * c + 1, p * LANES:(p + 1) * LANES] for b, p in pairs],
                                [st_ref[b, p] for b, p in pairs], masks)
    for (b, p), s in zip(pairs, s_new):
        st_ref[b, p] = s
    y = jnp.concatenate([jnp.concatenate(ypairs[b * npair:(b + 1) * npair], axis=1) for b in range(nb)],
                        axis=0)

    (ysum,) = _segsum([y], ones256)
    d = y - ysum * (1.0 / HEAD_DIM)
    (dsq,) = _segsum([d * d], ones256)
    yn = d * lax.rsqrt(dsq * (1.0 / HEAD_DIM) + GN_EPS) * gn_g + gn_b
    ob = yn + rk * v
    ub = (ob * _silu(z)).astype(BF16)
    for b in range(nb):
        o_ref[b] = ub[b * c:(b + 1) * c]


def _rwkv(hb, vecs, wcomb, ones256):
    b, s, wcols = hb.shape
    c = RWKV_CHUNK
    assert c & (c - 1) == 0 and 2 * c == LANES
    full = lambda a: pl.BlockSpec(a.shape, lambda i: (0,) * a.ndim)
    return pl.pallas_call(
        _rwkv_kernel,
        grid=(s // c,),
        in_specs=[pl.BlockSpec((b, c, wcols), lambda i: (0, i, 0)), full(vecs), full(wcomb), full(ones256)],
        out_specs=pl.BlockSpec((b, c, RWKV_WIDTH), lambda i: (0, i, 0)),
        out_shape=jax.ShapeDtypeStruct((b, s, RWKV_WIDTH), BF16),
        scratch_shapes=[pltpu.VMEM((b, RWKV_WIDTH // LANES, LANES, LANES), F32)],
        compiler_params=pltpu.CompilerParams(dimension_semantics=("arbitrary",),
                                             vmem_limit_bytes=VMEM_LIMIT),
        name="rwkv",
    )(hb, vecs, wcomb, ones256)


def _swa_kernel(sink_ref, q_ref, kp_ref, kc_ref, vp_ref, vc_ref, z_ref, o_ref):
    n = pl.program_id(1)
    bq = q_ref.shape[0]
    q = q_ref[...].astype(F32)
    kband = jnp.concatenate([kp_ref[...], kc_ref[...]], axis=0).astype(F32)
    vband = jnp.concatenate([vp_ref[...], vc_ref[...]], axis=0).astype(F32)
    rows = SWA_GROUP * bq
    r = lax.broadcasted_iota(jnp.int32, (rows, 2 * bq), 0)
    cc = lax.broadcasted_iota(jnp.int32, (rows, 2 * bq), 1)
    d = cc - (r & (bq - 1))
    lo_ok = jnp.where(n > 0, 0, bq)
    d = jnp.where(cc >= lo_ok, d, 0)
    r1 = lax.broadcasted_iota(jnp.int32, (rows, 1), 0)
    outs = []
    for g in range(SWA_KV_WIDTH // HEAD_DIM):
        kg = kband[:, g * HEAD_DIM:(g + 1) * HEAD_DIM].astype(BF16)
        vg = vband[:, g * HEAD_DIM:(g + 1) * HEAD_DIM].astype(BF16)
        heads = range(g * SWA_GROUP, (g + 1) * SWA_GROUP)
        qg = jnp.concatenate([q[:, j * HEAD_DIM:(j + 1) * HEAD_DIM] for j in heads], axis=0).astype(BF16)
        s = _dot_nt(qg, kg)
        s = jnp.where(d >= 1, jnp.where(d <= bq, s, NEG), NEG)
        sink = jnp.zeros((rows, 1), F32)
        for jj, j in enumerate(heads):
            sink = jnp.where(r1 >= jj * bq, sink_ref[j], sink)
        mx = jnp.maximum(jnp.max(s, axis=1, keepdims=True), sink)
        p = jnp.exp(s - mx)
        denom = jnp.sum(p, axis=1, keepdims=True) + jnp.exp(sink - mx)
        outs.append(_dot((p / denom).astype(BF16), vg))
    oc = jnp.concatenate([outs[g][jj * bq:(jj + 1) * bq]
                          for g in range(len(outs)) for jj in range(SWA_GROUP)], axis=1)
    o_ref[...] = (oc * _silu(z_ref[...])).astype(BF16)


def _swa(sinks, qc, kc, vc, zc):
    b, s, _ = qc.shape
    bq = SWA_BLOCK
    assert bq & (bq - 1) == 0
    cur = lambda w: pl.BlockSpec((None, bq, w), lambda bi, n, sk: (bi, n, 0))
    prv = lambda w: pl.BlockSpec((None, bq, w), lambda bi, n, sk: (bi, jnp.maximum(n - 1, 0), 0))
    return pl.pallas_call(
        _swa_kernel,
        grid_spec=pltpu.PrefetchScalarGridSpec(
            num_scalar_prefetch=1,
            grid=(b, s // bq),
            in_specs=[cur(SWA_WIDTH), prv(SWA_KV_WIDTH), cur(SWA_KV_WIDTH),
                      prv(SWA_KV_WIDTH), cur(SWA_KV_WIDTH), cur(SWA_WIDTH)],
            out_specs=cur(SWA_WIDTH)),
        out_shape=jax.ShapeDtypeStruct((b, s, SWA_WIDTH), BF16),
        compiler_params=pltpu.CompilerParams(dimension_semantics=("arbitrary",) * 2,
                                             vmem_limit_bytes=VMEM_LIMIT),
        name="swa",
    )(sinks, qc, kc, kc, vc, vc, zc)


def _out_kernel(x_ref, ua_ref, ub_ref, uc_ref, wg_ref, wa_ref, wb_ref, wc_ref, wo_ref, lng_ref, lnb_ref,
                o_ref, *, alpha):
    x = x_ref[...]
    d = x.shape[1]
    g = _dot(x.astype(BF16), wg_ref[...])
    merged = (jax.nn.sigmoid(g[:, 0:d]) * _dot(ua_ref[...], wa_ref[...])
              + jax.nn.sigmoid(g[:, d:2 * d]) * _dot(ub_ref[...], wb_ref[...])
              + jax.nn.sigmoid(g[:, 2 * d:]) * _dot(uc_ref[...], wc_ref[...]))
    y = alpha * x + _dot(merged.astype(BF16), wo_ref[...])
    mu = jnp.mean(y, axis=-1, keepdims=True)
    dev = y - mu
    var = jnp.mean(dev * dev, axis=-1, keepdims=True)
    o_ref[...] = dev * lax.rsqrt(var + LN_EPS) * lng_ref[...] + lnb_ref[...]


def _out(x2, ua, ub, uc, wg, wa, wb, wc, wo, lng, lnb, alpha, tm=256):
    t, d = x2.shape
    row = lambda w: pl.BlockSpec((tm, w), lambda i: (i, 0))
    full = lambda a: pl.BlockSpec(a.shape, lambda i: (0,) * a.ndim)
    return pl.pallas_call(
        functools.partial(_out_kernel, alpha=alpha),
        grid=(t // tm,),
        in_specs=[row(d), row(ua.shape[1]), row(ub.shape[1]), row(uc.shape[1]),
                  full(wg), full(wa), full(wb), full(wc), full(wo), full(lng), full(lnb)],
        out_specs=row(d),
        out_shape=jax.ShapeDtypeStruct((t, d), F32),
        compiler_params=pltpu.CompilerParams(dimension_semantics=("arbitrary",),
                                             vmem_limit_bytes=VMEM_LIMIT),
        name="merge_out",
    )(x2, ua, ub, uc, wg, wa, wb, wc, wo, lng, lnb)


def kernel(x, positions, w_in, mu_rwkv, w0, w_up, a0, a_up, k_k, k_a, r_k, gn_g, gn_b, sinks,
           w_branch_a, w_branch_b, w_branch_c, w_out, ln_g, ln_b):
    b, s, d = x.shape
    depth = w_in.shape[0]
    t = b * s
    alpha = (2 * depth) ** 0.25
    assert w_in.shape[2] == A_COLS + B_COLS + C_COLS + 3 * d

    half = HEAD_DIM // 2
    inv = jnp.power(ROPE_THETA, -jnp.arange(half, dtype=F32) / half)
    inv_row = jnp.tile(inv, LANES // half).reshape(1, LANES)
    cos, sin = _rope_tables(positions.reshape(t, 1), inv_row)

    head_of_lane = jnp.arange(256) // HEAD_DIM
    ones256 = (head_of_lane[:, None] == head_of_lane[None, :]).astype(BF16)

    x2 = x.reshape(t, d)
    for l in range(depth):
        wl = w_in[l].astype(BF16)
        wa = wl[:, 0:A_COLS]
        wb = wl[:, A_COLS:A_COLS + B_COLS]
        wc = wl[:, A_COLS + B_COLS:A_COLS + B_COLS + C_COLS]
        wg = wl[:, A_COLS + B_COLS + C_COLS:]
        qa, ka, vta, za, kmean, hb, qc, kc, vc, zc = _proj(
            x2, cos, sin, wa, wb, wc, mu_rwkv[l].reshape(1, B_COLS), s)

        nblk = s // MOBA_BLOCK
        ua = _moba(qa.reshape(b, s, MOBA_WIDTH), ka.reshape(b, s, MOBA_WIDTH),
                   vta.reshape(b, nblk, 4, LANES + 16, MOBA_BLOCK),
                   kmean.reshape(b, nblk, MOBA_WIDTH), za.reshape(b, s, MOBA_WIDTH))

        zeros_lora = jnp.zeros((LORA, RWKV_WIDTH), F32)
        wcomb = jnp.concatenate([jnp.concatenate([w_up[l], zeros_lora], axis=1),
                                 jnp.concatenate([zeros_lora, a_up[l]], axis=1)], axis=0).astype(BF16)
        vecs = jnp.stack([w0[l], a0[l], k_k[l], k_a[l], r_k[l].reshape(-1), gn_g[l], gn_b[l],
                          jnp.zeros((RWKV_WIDTH,), F32)], axis=0)
        ub = _rwkv(hb.reshape(b, s, B_COLS), vecs, wcomb, ones256)

        uc = _swa(sinks[l], qc.reshape(b, s, SWA_WIDTH), kc.reshape(b, s, SWA_KV_WIDTH),
                  vc.reshape(b, s, SWA_KV_WIDTH), zc.reshape(b, s, SWA_WIDTH))

        x2 = _out(x2, ua.reshape(t, MOBA_WIDTH), ub.reshape(t, RWKV_WIDTH), uc.reshape(t, SWA_WIDTH),
                  wg, w_branch_a[l].astype(BF16), w_branch_b[l].astype(BF16), w_branch_c[l].astype(BF16),
                  w_out[l].astype(BF16), ln_g[l].reshape(1, d), ln_b[l].reshape(1, d), alpha)
    return x2.reshape(b, s, d)
```

```python
import functools
import math

import jax
import jax.numpy as jnp
from jax import lax
from jax.experimental import pallas as pl
from jax.experimental.pallas import tpu as pltpu

F32 = jnp.float32
BF16 = jnp.bfloat16

HEAD_DIM = 64
LANES = 128
MOBA_BLOCK = 256
MOBA_TOPK = 3
MOBA_PAIRS_PER_STEP = 2
MOBA_WIDTH = 512
RWKV_WIDTH = 512
LORA = 64
RWKV_CHUNK = 64
SWA_WIDTH = 512
SWA_KV_WIDTH = 128
SWA_GROUP = 4
SWA_BLOCK = 128
ROPE_THETA = 10000.0
LN_EPS = 1e-5
GN_EPS = HEAD_DIM * 1e-5
DECAY_SCALE = math.exp(-0.5)
LOG2E = math.log2(math.e)
NEG = -1e30
VMEM_LIMIT = 56 * 1024 * 1024

A_COLS = 4 * MOBA_WIDTH
B_COLS = 3 * RWKV_WIDTH + 2 * LORA + RWKV_WIDTH
C_COLS = SWA_WIDTH + 2 * SWA_KV_WIDTH + SWA_WIDTH


def _dot(a, b):
    return jnp.dot(a, b, preferred_element_type=F32)


def _dot_nt(a, b):
    return lax.dot_general(a, b, (((1,), (1,)), ((), ())), preferred_element_type=F32)


def _dot_tn(a, b):
    return lax.dot_general(a, b, (((0,), (0,)), ((), ())), preferred_element_type=F32)


def _silu(z):
    return z * jax.nn.sigmoid(z)


def _split2(x):
    hi = x.astype(BF16)
    lo = (x - hi.astype(F32)).astype(BF16)
    return hi, lo


def _rope_table_kernel(pos_ref, inv_ref, cos_ref, sin_ref):
    ang = pos_ref[...].astype(F32) * inv_ref[...]
    lane = lax.broadcasted_iota(jnp.int32, ang.shape, 1)
    s = jnp.sin(ang)
    cos_ref[...] = jnp.cos(ang)
    sin_ref[...] = jnp.where((lane & 32) == 0, -s, s)


def _rope_tables(pos_col, inv_row, tm=1024):
    t = pos_col.shape[0]
    return pl.pallas_call(
        _rope_table_kernel,
        grid=(t // tm,),
        in_specs=[pl.BlockSpec((tm, 1), lambda i: (i, 0)),
                  pl.BlockSpec((1, LANES), lambda i: (0, 0))],
        out_specs=[pl.BlockSpec((tm, LANES), lambda i: (i, 0))] * 2,
        out_shape=[jax.ShapeDtypeStruct((t, LANES), F32)] * 2,
        name="rope_table",
    )(pos_col, inv_row)


def _proj_kernel(x_ref, cos_ref, sin_ref, wa_ref, wb_ref, wc_ref, mu_ref,
                 qa_ref, ka_ref, vta_ref, za_ref, kmean_ref, hb_ref,
                 qc_ref, kc_ref, vc_ref, zc_ref, carry_ref, *, tiles_per_seq):
    i = pl.program_id(0)
    xb = x_ref[...].astype(BF16)
    tm = xb.shape[0]
    cos = cos_ref[...]
    sin = sin_ref[...]
    lane = lax.broadcasted_iota(jnp.int32, (tm, LANES), 1)
    first_half = (lane & 32) == 0

    def rope(blk):
        partner = jnp.where(first_half, pltpu.roll(blk, 96, 1), pltpu.roll(blk, 32, 1))
        return blk * cos + partner * sin

    scale = HEAD_DIM ** -0.5
    nblk = tm // MOBA_BLOCK

    ha = _dot(xb, wa_ref[...])
    for j in range(MOBA_WIDTH // LANES):
        sl = slice(j * LANES, (j + 1) * LANES)
        qa_ref[:, sl] = (rope(ha[:, sl]) * (scale * LOG2E)).astype(BF16)
        kj = rope(ha[:, MOBA_WIDTH + j * LANES:MOBA_WIDTH + (j + 1) * LANES])
        ka_ref[:, sl] = kj.astype(BF16)
        for r in range(nblk):
            kmean_ref[r, :, sl] = jnp.mean(kj[r * MOBA_BLOCK:(r + 1) * MOBA_BLOCK], axis=0, keepdims=True)
    va = ha[:, 2 * MOBA_WIDTH:3 * MOBA_WIDTH]
    ones_rows = jnp.ones((16, MOBA_BLOCK), BF16)
    for r in range(nblk):
        vt = va[r * MOBA_BLOCK:(r + 1) * MOBA_BLOCK].T
        for p in range(MOBA_WIDTH // LANES):
            vta_ref[r, p, 0:LANES, :] = vt[p * LANES:(p + 1) * LANES].astype(BF16)
            vta_ref[r, p, LANES:LANES + 16, :] = ones_rows
    za_ref[...] = ha[:, 3 * MOBA_WIDTH:]

    hb = _dot(xb, wb_ref[...])
    prev = pltpu.roll(hb, 1, 0)
    row = lax.broadcasted_iota(jnp.int32, hb.shape, 0)

    @pl.when(i % tiles_per_seq == 0)
    def _():
        carry_ref[...] = jnp.zeros(carry_ref.shape, F32)

    prev = jnp.where(row == 0, carry_ref[...], prev)
    hb_ref[...] = hb + (prev - hb) * mu_ref[...]
    carry_ref[...] = hb[tm - 1:tm, :]

    hc = _dot(xb, wc_ref[...])
    for j in range(SWA_WIDTH // LANES):
        sl = slice(j * LANES, (j + 1) * LANES)
        qc_ref[:, sl] = (rope(hc[:, sl]) * scale).astype(BF16)
    kc_ref[...] = rope(hc[:, SWA_WIDTH:SWA_WIDTH + SWA_KV_WIDTH]).astype(BF16)
    vc_ref[...] = hc[:, SWA_WIDTH + SWA_KV_WIDTH:SWA_WIDTH + 2 * SWA_KV_WIDTH].astype(BF16)
    zc_ref[...] = hc[:, SWA_WIDTH + 2 * SWA_KV_WIDTH:]


def _proj(x2, cos, sin, wa, wb, wc, mu, seq, tm=256):
    t, d = x2.shape
    nb_tile = tm // MOBA_BLOCK
    row = lambda w: pl.BlockSpec((tm, w), lambda i: (i, 0))
    full = lambda a: pl.BlockSpec(a.shape, lambda i: (0,) * a.ndim)
    out_shape = [
        jax.ShapeDtypeStruct((t, MOBA_WIDTH), BF16),
        jax.ShapeDtypeStruct((t, MOBA_WIDTH), BF16),
        jax.ShapeDtypeStruct((t // MOBA_BLOCK, 4, LANES + 16, MOBA_BLOCK), BF16),
        jax.ShapeDtypeStruct((t, MOBA_WIDTH), F32),
        jax.ShapeDtypeStruct((t // MOBA_BLOCK, 1, MOBA_WIDTH), F32),
        jax.ShapeDtypeStruct((t, B_COLS), F32),
        jax.ShapeDtypeStruct((t, SWA_WIDTH), BF16),
        jax.ShapeDtypeStruct((t, SWA_KV_WIDTH), BF16),
        jax.ShapeDtypeStruct((t, SWA_KV_WIDTH), BF16),
        jax.ShapeDtypeStruct((t, SWA_WIDTH), F32),
    ]
    out_specs = [
        row(MOBA_WIDTH), row(MOBA_WIDTH),
        pl.BlockSpec((nb_tile, 4, LANES + 16, MOBA_BLOCK), lambda i: (i, 0, 0, 0)),
        row(MOBA_WIDTH),
        pl.BlockSpec((nb_tile, 1, MOBA_WIDTH), lambda i: (i, 0, 0)),
        row(B_COLS), row(SWA_WIDTH), row(SWA_KV_WIDTH), row(SWA_KV_WIDTH), row(SWA_WIDTH),
    ]
    return pl.pallas_call(
        functools.partial(_proj_kernel, tiles_per_seq=seq // tm),
        grid=(t // tm,),
        in_specs=[row(d), row(LANES), row(LANES), full(wa), full(wb), full(wc), full(mu)],
        out_specs=out_specs,
        out_shape=out_shape,
        scratch_shapes=[pltpu.VMEM((1, B_COLS), F32)],
        compiler_params=pltpu.CompilerParams(dimension_semantics=("arbitrary",),
                                             vmem_limit_bytes=VMEM_LIMIT),
        name="proj",
    )(x2, cos, sin, wa, wb, wc, mu)


def _moba_kernel(q_ref, k_ref, vt_ref, kmean_ref, z_ref, o_ref, acc_ref):
    i = pl.program_id(1)
    nb, nblk, _ = kmean_ref.shape
    bq = q_ref.shape[1]
    npl = q_ref.shape[2] // LANES
    chains = [(b, p) for b in range(nb) for p in range(npl)]
    cs = range(len(chains))
    row = lax.broadcasted_iota(jnp.int32, (LANES, bq), 0)
    blk = lax.broadcasted_iota(jnp.int32, (nblk, 2 * bq), 0).astype(F32)
    i_f = i.astype(F32)
    past = blk < i_f
    qaug = []
    for b, p in chains:
        qt = q_ref[b, :, p * LANES:(p + 1) * LANES].astype(F32).T
        qtm = jnp.concatenate([jnp.where(row < HEAD_DIM, qt, 0.0),
                               jnp.where(row >= HEAD_DIM, qt, 0.0)], axis=1).astype(BF16)
        gate = _dot(kmean_ref[b, :, p * LANES:(p + 1) * LANES].astype(BF16), qtm)
        g = jnp.where(past, gate, -jnp.inf)
        bias = jnp.full(gate.shape, NEG, F32)
        for _ in range(MOBA_TOPK):
            mx = jnp.max(g, axis=0, keepdims=True)
            first = jnp.min(jnp.where(g == mx, blk, float(nblk)), axis=0, keepdims=True)
            hit = blk == first
            bias = jnp.where(hit, 0.0, bias)
            g = jnp.where(hit, -jnp.inf, g)
        bias = jnp.where(past, bias, jnp.where(blk == i_f, 0.0, NEG))
        qaug.append(jnp.concatenate([qtm, bias.astype(BF16),
                                     jnp.zeros((LANES - nblk, 2 * bq), BF16)], axis=0))

    acc_ref[...] = jnp.zeros(acc_ref.shape, F32)
    krow = lax.broadcasted_iota(jnp.int32, (2 * MOBA_BLOCK, LANES), 0)
    klane = lax.broadcasted_iota(jnp.int32, (2 * MOBA_BLOCK, LANES), 1)
    shift = MOBA_BLOCK.bit_length() - 1

    def visit(n, m, mask):
        onehot = jnp.where(klane == n + (krow >> shift), 1.0, 0.0).astype(BF16)
        off = pl.multiple_of(n * MOBA_BLOCK, 2 * MOBA_BLOCK)
        st = [_dot(jnp.concatenate([k_ref[b, pl.ds(off, 2 * MOBA_BLOCK), p * LANES:(p + 1) * LANES], onehot],
                                   axis=1), qaug[c]) for c, (b, p) in enumerate(chains)]
        if mask is not None:
            st = [jnp.where(mask, st[c], NEG) for c in cs]
        m_new = [jnp.maximum(m[c], jnp.max(st[c], axis=0, keepdims=True)) for c in cs]
        alpha = [jnp.exp2(m[c] - m_new[c]) for c in cs]
        pt = [jnp.exp2(st[c] - m_new[c]).astype(BF16) for c in cs]
        pv = [_dot(jnp.concatenate([vt_ref[b, n, p], vt_ref[b, n + 1, p]], axis=1), pt[c])
              for c, (b, p) in enumerate(chains)]
        for c, (b, p) in enumerate(chains):
            acc_ref[b, p] = acc_ref[b, p] * alpha[c] + pv[c]
        return m_new

    m = lax.fori_loop(0, i // 2, lambda t, m: visit(2 * t, m, None),
                      [jnp.full((1, 2 * bq), NEG, F32) for _ in cs])
    last = 2 * (i // 2)
    kpos = last * MOBA_BLOCK + lax.broadcasted_iota(jnp.int32, (2 * MOBA_BLOCK, 2 * bq), 0)
    qpos = i * bq + (lax.broadcasted_iota(jnp.int32, (2 * MOBA_BLOCK, 2 * bq), 1) & (bq - 1))
    visit(last, m, kpos <= qpos)

    lane_o = lax.broadcasted_iota(jnp.int32, (bq, LANES), 1)
    for b, p in chains:
        acc = acc_ref[b, p]
        ot = acc[0:LANES, :] / acc[LANES:LANES + 1, :]
        o = ot.T
        o2 = jnp.where(lane_o < HEAD_DIM, o[0:bq], o[bq:2 * bq])
        sl = slice(p * LANES, (p + 1) * LANES)
        o_ref[b, :, sl] = (o2 * _silu(z_ref[b, :, sl])).astype(BF16)


def _moba(qa, ka, vta, kmean, za):
    b, s, _ = qa.shape
    nblk = s // MOBA_BLOCK
    assert nblk <= LANES and nblk % 2 == 0 and MOBA_BLOCK & (MOBA_BLOCK - 1) == 0
    npl = MOBA_PAIRS_PER_STEP
    wl = npl * LANES
    tile = pl.BlockSpec((b, MOBA_BLOCK, wl), lambda p, i: (0, i, p))
    return pl.pallas_call(
        _moba_kernel,
        grid=(MOBA_WIDTH // wl, nblk),
        in_specs=[tile,
                  pl.BlockSpec((b, s, wl), lambda p, i: (0, 0, p)),
                  pl.BlockSpec((b, nblk, npl, LANES + 16, MOBA_BLOCK), lambda p, i: (0, 0, p, 0, 0)),
                  pl.BlockSpec((b, nblk, wl), lambda p, i: (0, 0, p)),
                  tile],
        out_specs=tile,
        out_shape=jax.ShapeDtypeStruct((b, s, MOBA_WIDTH), BF16),
        scratch_shapes=[pltpu.VMEM((b, npl, LANES + 16, 2 * MOBA_BLOCK), F32)],
        compiler_params=pltpu.CompilerParams(dimension_semantics=("arbitrary",) * 2,
                                             vmem_limit_bytes=VMEM_LIMIT),
        name="moba",
    )(qa, ka, vta, kmean, za)


def _segsum(xs, ones256):
    m = xs[0].shape[0]
    parts = []
    for x in xs:
        for piece in _split2(x):
            parts += [piece[:, :256], piece[:, 256:]]
    res = _dot(jnp.concatenate(parts, axis=0), ones256)
    outs = []
    for idx in range(len(xs)):
        r = [res[(idx * 4 + j) * m:(idx * 4 + j + 1) * m] for j in range(4)]
        outs.append(jnp.concatenate([r[0] + r[2], r[1] + r[3]], axis=1))
    return outs


def _rwkv_pairs(at, rt, kh, bh, kt, bt, v, ptot, s2, masks):
    head0, head0_wide, strict, incl, bdiag = masks
    c = at[0].shape[0]
    idx = range(len(at))
    lo = lambda t: jnp.where(head0, t, 0.0)
    hi = lambda t: jnp.where(head0, 0.0, t)
    split = lambda t: jnp.concatenate([lo(t), hi(t)], axis=0)
    lhs = [jnp.concatenate([at[i], rt[i]], axis=0).astype(BF16) for i in idx]
    rhs = [jnp.concatenate([split(bh[i]), split(kh[i])], axis=0).astype(BF16) for i in idx]
    sc = [_dot_nt(lhs[i], rhs[i]) for i in idx]
    p = [jnp.where(strict, sc[i][0:c], 0.0) for i in idx]
    mr = [jnp.where(incl, sc[i][c:2 * c], 0.0).astype(BF16) for i in idx]
    vs = [split(v[i]).astype(BF16) for i in idx]
    xu = [_dot(p[i][:, LANES:].astype(BF16), vs[i]) for i in idx]
    x = [jnp.concatenate([at[i], xu[i]], axis=1) for i in idx]
    p = [p[i][:, 0:LANES] for i in idx]
    steps = (c - 1).bit_length()
    for j in range(steps):
        pb = [p[i].astype(BF16) for i in idx]
        xs = [jnp.concatenate([jnp.where(head0_wide, x[i], 0.0), jnp.where(head0_wide, 0.0, x[i])],
                              axis=0).astype(BF16) for i in idx]
        x = [x[i] + _dot(pb[i], xs[i]) for i in idx]
        if j + 1 < steps:
            p = [_dot(pb[i], split(p[i]).astype(BF16)) for i in idx]
    wr = [_dot_nt(jnp.concatenate([x[i][:, 0:LANES], rt[i]], axis=0).astype(BF16), s2[i].astype(BF16))
          for i in idx]
    u = [wr[i][0:c] + x[i][:, LANES:] for i in idx]
    ym = [_dot(mr[i], jnp.concatenate([split(u[i]), split(v[i])], axis=0).astype(BF16)) for i in idx]
    y = [wr[i][c:2 * c] + ym[i] for i in idx]
    vu = [jnp.concatenate([u[i], v[i]], axis=0).astype(BF16) for i in idx]
    bk = [jnp.concatenate([bt[i], kt[i]], axis=0).astype(BF16) for i in idx]
    s2_new = [s2[i] * ptot[i] + jnp.where(bdiag, _dot_tn(vu[i], bk[i]), 0.0) for i in idx]
    return y, s2_new


def _rwkv_kernel(h_ref, vec_ref, wcomb_ref, ones_ref, o_ref, st_ref):
    ci = pl.program_id(0)
    nb, c, _ = h_ref.shape
    m = nb * c
    w = RWKV_WIDTH

    @pl.when(ci == 0)
    def _():
        st_ref[...] = jnp.zeros(st_ref.shape, F32)

    h = h_ref[...].reshape(m, h_ref.shape[2])
    r = h[:, 0:w]
    k = h[:, w:2 * w]
    v = h[:, 2 * w:3 * w]
    lo = h[:, 3 * w:3 * w + 2 * LORA]
    z = h[:, 3 * w + 2 * LORA:]
    w0, a0, k_k, k_a, r_k, gn_g, gn_b = [vec_ref[j:j + 1, :] for j in range(7)]
    ones256 = ones_ref[...]

    lane_m = lax.broadcasted_iota(jnp.int32, (m, LANES), 1)
    lo = jnp.where(lane_m < LORA, jnp.tanh(lo), lo)
    wa = _dot(lo.astype(BF16), wcomb_ref[...])
    logw = -DECAY_SCALE * jax.nn.sigmoid(w0 + wa[:, 0:w])
    a = jax.nn.sigmoid(a0 + wa[:, w:])
    kkr = k * k_k
    k2 = k * (1.0 + (a - 1.0) * k_a)
    ss, rk = _segsum([kkr * kkr, r * k2 * r_k], ones256)
    kk = kkr / jnp.maximum(jnp.sqrt(ss), 1e-12)
    aa = -kk
    bb = kk * a

    tr = lax.broadcasted_iota(jnp.int32, (m, m), 0)
    tc = lax.broadcasted_iota(jnp.int32, (m, m), 1)
    sh = c.bit_length() - 1
    tri = jnp.where(tr >= tc, jnp.where((tr >> sh) == (tc >> sh), 1.0, 0.0), 0.0).astype(BF16)
    cinc = sum(_dot(tri, piece) for piece in _split2(logw))
    ctot = jnp.concatenate([jnp.broadcast_to(cinc[(b + 1) * c - 1:(b + 1) * c], (c, w)) for b in range(nb)], axis=0)
    einv = jnp.exp(-cinc)
    edec = jnp.exp(ctot - cinc)
    at = aa * jnp.exp(cinc - logw)
    rt = r * jnp.exp(cinc)
    kh = k2 * einv
    bh = bb * einv
    kt = k2 * edec
    bt = bb * edec
    ptot = jnp.exp(ctot)

    r2 = lax.broadcasted_iota(jnp.int32, (2 * c, 2 * c), 0)
    c2 = lax.broadcasted_iota(jnp.int32, (2 * c, 2 * c), 1)
    ti = lax.broadcasted_iota(jnp.int32, (c, 4 * c), 0)
    tj = lax.broadcasted_iota(jnp.int32, (c, 4 * c), 1) & (c - 1)
    masks = (lax.broadcasted_iota(jnp.int32, (c, LANES), 1) < HEAD_DIM,
             (lax.broadcasted_iota(jnp.int32, (c, 2 * LANES), 1) & HEAD_DIM) == 0,
             ti > tj, ti >= tj, (r2 >> sh) == (c2 >> sh))

    npair = w // LANES
    pairs = [(b, p) for b in range(nb) for p in range(npair)]
    cut = lambda t: [t[b * c:(b + 1) * c, p * LANES:(p + 1) * LANES] for b, p in pairs]
    ypairs, s_new = _rwkv_pairs(cut(at), cut(rt), cut(kh), cut(bh), cut(kt), cut(bt), cut(v),
                                [ptot[b * c:b * c + 1, p * LANES:(p + 1) * LANES] for b, p in pairs],
                                [st_ref[b, p] for b, p in pairs], masks)
    for (b, p), s in zip(pairs, s_new):
        st_ref[b, p] = s
    y = jnp.concatenate([jnp.concatenate(ypairs[b * npair:(b + 1) * npair], axis=1) for b in range(nb)],
                        axis=0)

    (ysum,) = _segsum([y], ones256)
    d = y - ysum * (1.0 / HEAD_DIM)
    (dsq,) = _segsum([d * d], ones256)
    yn = d * lax.rsqrt(dsq * (1.0 / HEAD_DIM) + GN_EPS) * gn_g + gn_b
    ob = yn + rk * v
    ub = (ob * _silu(z)).astype(BF16)
    for b in range(nb):
        o_ref[b] = ub[b * c:(b + 1) * c]


def _rwkv(hb, vecs, wcomb, ones256):
    b, s, wcols = hb.shape
    c = RWKV_CHUNK
    assert c & (c - 1) == 0 and 2 * c == LANES
    full = lambda a: pl.BlockSpec(a.shape, lambda i: (0,) * a.ndim)
    return pl.pallas_call(
        _rwkv_kernel,
        grid=(s // c,),
        in_specs=[pl.BlockSpec((b, c, wcols), lambda i: (0, i, 0)), full(vecs), full(wcomb), full(ones256)],
        out_specs=pl.BlockSpec((b, c, RWKV_WIDTH), lambda i: (0, i, 0)),
        out_shape=jax.ShapeDtypeStruct((b, s, RWKV_WIDTH), BF16),
        scratch_shapes=[pltpu.VMEM((b, RWKV_WIDTH // LANES, LANES, LANES), F32)],
        compiler_params=pltpu.CompilerParams(dimension_semantics=("arbitrary",),
                                             vmem_limit_bytes=VMEM_LIMIT),
        name="rwkv",
    )(hb, vecs, wcomb, ones256)


def _swa_kernel(sink_ref, q_ref, kp_ref, kc_ref, vp_ref, vc_ref, z_ref, o_ref):
    n = pl.program_id(1)
    bq = q_ref.shape[0]
    cols = SWA_GROUP * bq
    qt = q_ref[...].astype(F32).T
    kband = jnp.concatenate([kp_ref[...], kc_ref[...]], axis=0)
    vt = jnp.concatenate([vp_ref[...], vc_ref[...]], axis=0).astype(F32).T.astype(BF16)
    kc = lax.broadcasted_iota(jnp.int32, (2 * bq, cols), 0)
    d = kc - (lax.broadcasted_iota(jnp.int32, (2 * bq, cols), 1) & (bq - 1))
    lo_ok = jnp.where(n > 0, 0, bq)
    d = jnp.where(kc >= lo_ok, d, 0)
    lane = lax.broadcasted_iota(jnp.int32, (1, cols), 1)
    zeros = jnp.zeros((HEAD_DIM, cols), F32)
    outs = []
    for g in range(SWA_KV_WIDTH // HEAD_DIM):
        heads = range(g * SWA_GROUP, (g + 1) * SWA_GROUP)
        qg = jnp.concatenate([qt[j * HEAD_DIM:(j + 1) * HEAD_DIM] for j in heads], axis=1)
        qaug = jnp.concatenate([qg, zeros] if g == 0 else [zeros, qg], axis=0).astype(BF16)
        st = _dot(kband, qaug)
        st = jnp.where(d >= 1, jnp.where(d <= bq, st, NEG), NEG)
        sink = jnp.zeros((1, cols), F32)
        for jj, j in enumerate(heads):
            sink = jnp.where(lane >= jj * bq, sink_ref[j], sink)
        mx = jnp.maximum(jnp.max(st, axis=0, keepdims=True), sink)
        p = jnp.exp(st - mx)
        denom = jnp.sum(p, axis=0, keepdims=True) + jnp.exp(sink - mx)
        ot = _dot(vt, (p / denom).astype(BF16))
        outs += [ot[g * HEAD_DIM:(g + 1) * HEAD_DIM, jj * bq:(jj + 1) * bq] for jj in range(SWA_GROUP)]
    oc = jnp.concatenate(outs, axis=0).T
    o_ref[...] = (oc * _silu(z_ref[...])).astype(BF16)


def _swa(sinks, qc, kc, vc, zc):
    b, s, _ = qc.shape
    bq = SWA_BLOCK
    assert bq & (bq - 1) == 0
    cur = lambda w: pl.BlockSpec((None, bq, w), lambda bi, n, sk: (bi, n, 0))
    prv = lambda w: pl.BlockSpec((None, bq, w), lambda bi, n, sk: (bi, jnp.maximum(n - 1, 0), 0))
    return pl.pallas_call(
        _swa_kernel,
        grid_spec=pltpu.PrefetchScalarGridSpec(
            num_scalar_prefetch=1,
            grid=(b, s // bq),
            in_specs=[cur(SWA_WIDTH), prv(SWA_KV_WIDTH), cur(SWA_KV_WIDTH),
                      prv(SWA_KV_WIDTH), cur(SWA_KV_WIDTH), cur(SWA_WIDTH)],
            out_specs=cur(SWA_WIDTH)),
        out_shape=jax.ShapeDtypeStruct((b, s, SWA_WIDTH), BF16),
        compiler_params=pltpu.CompilerParams(dimension_semantics=("arbitrary",) * 2,
                                             vmem_limit_bytes=VMEM_LIMIT),
        name="swa",
    )(sinks, qc, kc, kc, vc, vc, zc)


def _out_kernel(x_ref, ua_ref, ub_ref, uc_ref, wg_ref, wa_ref, wb_ref, wc_ref, wo_ref, lng_ref, lnb_ref,
                o_ref, *, alpha):
    x = x_ref[...]
    d = x.shape[1]
    g = _dot(x.astype(BF16), wg_ref[...])
    merged = (jax.nn.sigmoid(g[:, 0:d]) * _dot(ua_ref[...], wa_ref[...])
              + jax.nn.sigmoid(g[:, d:2 * d]) * _dot(ub_ref[...], wb_ref[...])
              + jax.nn.sigmoid(g[:, 2 * d:]) * _dot(uc_ref[...], wc_ref[...]))
    y = alpha * x + _dot(merged.astype(BF16), wo_ref[...])
    mu = jnp.mean(y, axis=-1, keepdims=True)
    dev = y - mu
    var = jnp.mean(dev * dev, axis=-1, keepdims=True)
    o_ref[...] = dev * lax.rsqrt(var + LN_EPS) * lng_ref[...] + lnb_ref[...]


def _out(x2, ua, ub, uc, wg, wa, wb, wc, wo, lng, lnb, alpha, tm=256):
    t, d = x2.shape
    row = lambda w: pl.BlockSpec((tm, w), lambda i: (i, 0))
    full = lambda a: pl.BlockSpec(a.shape, lambda i: (0,) * a.ndim)
    return pl.pallas_call(
        functools.partial(_out_kernel, alpha=alpha),
        grid=(t // tm,),
        in_specs=[row(d), row(ua.shape[1]), row(ub.shape[1]), row(uc.shape[1]),
                  full(wg), full(wa), full(wb), full(wc), full(wo), full(lng), full(lnb)],
        out_specs=row(d),
        out_shape=jax.ShapeDtypeStruct((t, d), F32),
        compiler_params=pltpu.CompilerParams(dimension_semantics=("arbitrary",),
                                             vmem_limit_bytes=VMEM_LIMIT),
        name="merge_out",
    )(x2, ua, ub, uc, wg, wa, wb, wc, wo, lng, lnb)


def kernel(x, positions, w_in, mu_rwkv, w0, w_up, a0, a_up, k_k, k_a, r_k, gn_g, gn_b, sinks,
           w_branch_a, w_branch_b, w_branch_c, w_out, ln_g, ln_b):
    b, s, d = x.shape
    depth = w_in.shape[0]
    t = b * s
    alpha = (2 * depth) ** 0.25
    assert w_in.shape[2] == A_COLS + B_COLS + C_COLS + 3 * d

    half = HEAD_DIM // 2
    inv = jnp.power(ROPE_THETA, -jnp.arange(half, dtype=F32) / half)
    inv_row = jnp.tile(inv, LANES // half).reshape(1, LANES)
    cos, sin = _rope_tables(positions.reshape(t, 1), inv_row)

    head_of_lane = jnp.arange(256) // HEAD_DIM
    ones256 = (head_of_lane[:, None] == head_of_lane[None, :]).astype(BF16)

    x2 = x.reshape(t, d)
    for l in range(depth):
        wl = w_in[l].astype(BF16)
        wa = wl[:, 0:A_COLS]
        wb = wl[:, A_COLS:A_COLS + B_COLS]
        wc = wl[:, A_COLS + B_COLS:A_COLS + B_COLS + C_COLS]
        wg = wl[:, A_COLS + B_COLS + C_COLS:]
        qa, ka, vta, za, kmean, hb, qc, kc, vc, zc = _proj(
            x2, cos, sin, wa, wb, wc, mu_rwkv[l].reshape(1, B_COLS), s)

        nblk = s // MOBA_BLOCK
        ua = _moba(qa.reshape(b, s, MOBA_WIDTH), ka.reshape(b, s, MOBA_WIDTH),
                   vta.reshape(b, nblk, 4, LANES + 16, MOBA_BLOCK),
                   kmean.reshape(b, nblk, MOBA_WIDTH), za.reshape(b, s, MOBA_WIDTH))

        zeros_lora = jnp.zeros((LORA, RWKV_WIDTH), F32)
        wcomb = jnp.concatenate([jnp.concatenate([w_up[l], zeros_lora], axis=1),
                                 jnp.concatenate([zeros_lora, a_up[l]], axis=1)], axis=0).astype(BF16)
        vecs = jnp.stack([w0[l], a0[l], k_k[l], k_a[l], r_k[l].reshape(-1), gn_g[l], gn_b[l],
                          jnp.zeros((RWKV_WIDTH,), F32)], axis=0)
        ub = _rwkv(hb.reshape(b, s, B_COLS), vecs, wcomb, ones256)

        uc = _swa(sinks[l], qc.reshape(b, s, SWA_WIDTH), kc.reshape(b, s, SWA_KV_WIDTH),
                  vc.reshape(b, s, SWA_KV_WIDTH), zc.reshape(b, s, SWA_WIDTH))

        x2 = _out(x2, ua.reshape(t, MOBA_WIDTH), ub.reshape(t, RWKV_WIDTH), uc.reshape(t, SWA_WIDTH),
                  wg, w_branch_a[l].astype(BF16), w_branch_b[l].astype(BF16), w_branch_c[l].astype(BF16),
                  w_out[l].astype(BF16), ln_g[l].reshape(1, d), ln_b[l].reshape(1, d), alpha)
    return x2.reshape(b, s, d)
```

```python
import functools
import math

import jax
import jax.numpy as jnp
from jax import lax
from jax.experimental import pallas as pl
from jax.experimental.pallas import tpu as pltpu

F32 = jnp.float32
BF16 = jnp.bfloat16

HEAD_DIM = 64
LANES = 128
MOBA_BLOCK = 256
MOBA_TOPK = 3
MOBA_PAIRS_PER_STEP = 2
MOBA_VROWS = HEAD_DIM + 16
MOBA_WIDTH = 512
RWKV_WIDTH = 512
LORA = 64
RWKV_CHUNK = 64
SWA_WIDTH = 512
SWA_KV_WIDTH = 128
SWA_GROUP = 4
SWA_BLOCK = 128
ROPE_THETA = 10000.0
LN_EPS = 1e-5
GN_EPS = HEAD_DIM * 1e-5
DECAY_SCALE = math.exp(-0.5)
LOG2E = math.log2(math.e)
NEG = -1e30
VMEM_LIMIT = 56 * 1024 * 1024
ROW_TILE = 512

A_COLS = 4 * MOBA_WIDTH
B_COLS = 3 * RWKV_WIDTH + 2 * LORA + RWKV_WIDTH
C_COLS = SWA_WIDTH + 2 * SWA_KV_WIDTH + SWA_WIDTH


def _dot(a, b):
    return jnp.dot(a, b, preferred_element_type=F32)


def _dot_nt(a, b):
    return lax.dot_general(a, b, (((1,), (1,)), ((), ())), preferred_element_type=F32)


def _dot_tn(a, b):
    return lax.dot_general(a, b, (((0,), (0,)), ((), ())), preferred_element_type=F32)


def _silu(z):
    return z * jax.nn.sigmoid(z)


def _split2(x):
    hi = x.astype(BF16)
    lo = (x - hi.astype(F32)).astype(BF16)
    return hi, lo


def _rope_table_kernel(pos_ref, inv_ref, cos_ref, sin_ref):
    ang = pos_ref[...].astype(F32) * inv_ref[...]
    lane = lax.broadcasted_iota(jnp.int32, ang.shape, 1)
    s = jnp.sin(ang)
    cos_ref[...] = jnp.cos(ang)
    sin_ref[...] = jnp.where((lane & 32) == 0, -s, s)


def _rope_tables(pos_col, inv_row, tm=1024):
    t = pos_col.shape[0]
    return pl.pallas_call(
        _rope_table_kernel,
        grid=(t // tm,),
        in_specs=[pl.BlockSpec((tm, 1), lambda i: (i, 0)),
                  pl.BlockSpec((1, LANES), lambda i: (0, 0))],
        out_specs=[pl.BlockSpec((tm, LANES), lambda i: (i, 0))] * 2,
        out_shape=[jax.ShapeDtypeStruct((t, LANES), F32)] * 2,
        name="rope_table",
    )(pos_col, inv_row)


def _proj_kernel(x_ref, cos_ref, sin_ref, wa_ref, wb_ref, wc_ref, mu_ref,
                 qa_ref, ka_ref, vta_ref, za_ref, kmean_ref, hb_ref,
                 qc_ref, kc_ref, vc_ref, zc_ref, carry_ref, *, tiles_per_seq):
    i = pl.program_id(0)
    xb = x_ref[...].astype(BF16)
    tm = xb.shape[0]
    cos = cos_ref[...]
    sin = sin_ref[...]
    lane = lax.broadcasted_iota(jnp.int32, (tm, LANES), 1)
    first_half = (lane & 32) == 0

    def rope(blk):
        partner = jnp.where(first_half, pltpu.roll(blk, 96, 1), pltpu.roll(blk, 32, 1))
        return blk * cos + partner * sin

    scale = HEAD_DIM ** -0.5
    nblk = tm // MOBA_BLOCK

    ha = _dot(xb, wa_ref[...])
    for j in range(MOBA_WIDTH // LANES):
        sl = slice(j * LANES, (j + 1) * LANES)
        qa_ref[:, sl] = (rope(ha[:, sl]) * (scale * LOG2E)).astype(BF16)
        kj = rope(ha[:, MOBA_WIDTH + j * LANES:MOBA_WIDTH + (j + 1) * LANES])
        ka_ref[:, sl] = kj.astype(BF16)
        for r in range(nblk):
            kmean_ref[r, :, sl] = jnp.mean(kj[r * MOBA_BLOCK:(r + 1) * MOBA_BLOCK], axis=0, keepdims=True)
    va = ha[:, 2 * MOBA_WIDTH:3 * MOBA_WIDTH]
    ones_rows = jnp.ones((16, MOBA_BLOCK), BF16)
    for r in range(nblk):
        vt = va[r * MOBA_BLOCK:(r + 1) * MOBA_BLOCK].T
        for hd in range(MOBA_WIDTH // HEAD_DIM):
            vta_ref[r, hd, 0:HEAD_DIM, :] = vt[hd * HEAD_DIM:(hd + 1) * HEAD_DIM].astype(BF16)
            vta_ref[r, hd, HEAD_DIM:MOBA_VROWS, :] = ones_rows
    za_ref[...] = ha[:, 3 * MOBA_WIDTH:]

    hb = _dot(xb, wb_ref[...])
    prev = pltpu.roll(hb, 1, 0)
    row = lax.broadcasted_iota(jnp.int32, hb.shape, 0)

    @pl.when(i % tiles_per_seq == 0)
    def _():
        carry_ref[...] = jnp.zeros(carry_ref.shape, F32)

    prev = jnp.where(row == 0, carry_ref[...], prev)
    hb_ref[...] = hb + (prev - hb) * mu_ref[...]
    carry_ref[...] = hb[tm - 1:tm, :]

    hc = _dot(xb, wc_ref[...])
    for j in range(SWA_WIDTH // LANES):
        sl = slice(j * LANES, (j + 1) * LANES)
        qc_ref[:, sl] = (rope(hc[:, sl]) * scale).astype(BF16)
    kc_ref[...] = rope(hc[:, SWA_WIDTH:SWA_WIDTH + SWA_KV_WIDTH]).astype(BF16)
    vc_ref[...] = hc[:, SWA_WIDTH + SWA_KV_WIDTH:SWA_WIDTH + 2 * SWA_KV_WIDTH].astype(BF16)
    zc_ref[...] = hc[:, SWA_WIDTH + 2 * SWA_KV_WIDTH:]


def _proj(x2, cos, sin, wa, wb, wc, mu, seq, tm=ROW_TILE):
    t, d = x2.shape
    nb_tile = tm // MOBA_BLOCK
    row = lambda w: pl.BlockSpec((tm, w), lambda i: (i, 0))
    full = lambda a: pl.BlockSpec(a.shape, lambda i: (0,) * a.ndim, pipeline_mode=pl.Buffered(1))
    out_shape = [
        jax.ShapeDtypeStruct((t, MOBA_WIDTH), BF16),
        jax.ShapeDtypeStruct((t, MOBA_WIDTH), BF16),
        jax.ShapeDtypeStruct((t // MOBA_BLOCK, MOBA_WIDTH // HEAD_DIM, MOBA_VROWS, MOBA_BLOCK), BF16),
        jax.ShapeDtypeStruct((t, MOBA_WIDTH), F32),
        jax.ShapeDtypeStruct((t // MOBA_BLOCK, 1, MOBA_WIDTH), F32),
        jax.ShapeDtypeStruct((t, B_COLS), F32),
        jax.ShapeDtypeStruct((t, SWA_WIDTH), BF16),
        jax.ShapeDtypeStruct((t, SWA_KV_WIDTH), BF16),
        jax.ShapeDtypeStruct((t, SWA_KV_WIDTH), BF16),
        jax.ShapeDtypeStruct((t, SWA_WIDTH), F32),
    ]
    out_specs = [
        row(MOBA_WIDTH), row(MOBA_WIDTH),
        pl.BlockSpec((nb_tile, MOBA_WIDTH // HEAD_DIM, MOBA_VROWS, MOBA_BLOCK), lambda i: (i, 0, 0, 0)),
        row(MOBA_WIDTH),
        pl.BlockSpec((nb_tile, 1, MOBA_WIDTH), lambda i: (i, 0, 0)),
        row(B_COLS), row(SWA_WIDTH), row(SWA_KV_WIDTH), row(SWA_KV_WIDTH), row(SWA_WIDTH),
    ]
    return pl.pallas_call(
        functools.partial(_proj_kernel, tiles_per_seq=seq // tm),
        grid=(t // tm,),
        in_specs=[row(d), row(LANES), row(LANES), full(wa), full(wb), full(wc), full(mu)],
        out_specs=out_specs,
        out_shape=out_shape,
        scratch_shapes=[pltpu.VMEM((1, B_COLS), F32)],
        compiler_params=pltpu.CompilerParams(dimension_semantics=("arbitrary",),
                                             vmem_limit_bytes=VMEM_LIMIT),
        name="proj",
    )(x2, cos, sin, wa, wb, wc, mu)


def _moba_kernel(q_ref, k_ref, vt_ref, kmean_ref, z_ref, o_ref, acc_ref):
    i = pl.program_id(1)
    nb, nblk, _ = kmean_ref.shape
    bq = q_ref.shape[1]
    npl = q_ref.shape[2] // LANES
    chains = [(b, p) for b in range(nb) for p in range(npl)]
    cs = range(len(chains))
    row = lax.broadcasted_iota(jnp.int32, (LANES, bq), 0)
    blk = lax.broadcasted_iota(jnp.int32, (nblk, 2 * bq), 0).astype(F32)
    i_f = i.astype(F32)
    past = blk < i_f
    qaug = []
    for b, p in chains:
        qt = q_ref[b, :, p * LANES:(p + 1) * LANES].astype(F32).T
        qtm = jnp.concatenate([jnp.where(row < HEAD_DIM, qt, 0.0),
                               jnp.where(row >= HEAD_DIM, qt, 0.0)], axis=1).astype(BF16)
        gate = _dot(kmean_ref[b, :, p * LANES:(p + 1) * LANES].astype(BF16), qtm)
        g = jnp.where(past, gate, -jnp.inf)
        bias = jnp.full(gate.shape, NEG, F32)
        for _ in range(MOBA_TOPK):
            mx = jnp.max(g, axis=0, keepdims=True)
            first = jnp.min(jnp.where(g == mx, blk, float(nblk)), axis=0, keepdims=True)
            hit = blk == first
            bias = jnp.where(hit, 0.0, bias)
            g = jnp.where(hit, -jnp.inf, g)
        bias = jnp.where(past, bias, jnp.where(blk == i_f, 0.0, NEG))
        qaug.append(jnp.concatenate([qtm, bias.astype(BF16),
                                     jnp.zeros((LANES - nblk, 2 * bq), BF16)], axis=0))

    acc_ref[...] = jnp.zeros(acc_ref.shape, F32)
    krow = lax.broadcasted_iota(jnp.int32, (2 * MOBA_BLOCK, LANES), 0)
    klane = lax.broadcasted_iota(jnp.int32, (2 * MOBA_BLOCK, LANES), 1)
    shift = MOBA_BLOCK.bit_length() - 1

    def visit(n, m, mask):
        onehot = jnp.where(klane == n + (krow >> shift), 1.0, 0.0).astype(BF16)
        off = pl.multiple_of(n * MOBA_BLOCK, 2 * MOBA_BLOCK)
        st = [_dot(jnp.concatenate([k_ref[b, pl.ds(off, 2 * MOBA_BLOCK), p * LANES:(p + 1) * LANES], onehot],
                                   axis=1), qaug[c]) for c, (b, p) in enumerate(chains)]
        if mask is not None:
            st = [jnp.where(mask, st[c], NEG) for c in cs]
        m_new = [jnp.maximum(m[c], jnp.max(st[c], axis=0, keepdims=True)) for c in cs]
        alpha = [jnp.exp2(m[c] - m_new[c]) for c in cs]
        pt = [jnp.exp2(st[c] - m_new[c]).astype(BF16) for c in cs]
        for hd in range(2):
            qs = slice(hd * bq, (hd + 1) * bq)
            pv = [_dot(jnp.concatenate([vt_ref[b, n, 2 * p + hd], vt_ref[b, n + 1, 2 * p + hd]], axis=1),
                       pt[c][:, qs]) for c, (b, p) in enumerate(chains)]
            for c, (b, p) in enumerate(chains):
                acc_ref[b, 2 * p + hd] = acc_ref[b, 2 * p + hd] * alpha[c][:, qs] + pv[c]
        return m_new

    m = lax.fori_loop(0, i // 2, lambda t, m: visit(2 * t, m, None),
                      [jnp.full((1, 2 * bq), NEG, F32) for _ in cs])
    last = 2 * (i // 2)
    kpos = last * MOBA_BLOCK + lax.broadcasted_iota(jnp.int32, (2 * MOBA_BLOCK, 2 * bq), 0)
    qpos = i * bq + (lax.broadcasted_iota(jnp.int32, (2 * MOBA_BLOCK, 2 * bq), 1) & (bq - 1))
    visit(last, m, kpos <= qpos)

    for b, p in chains:
        ot = []
        for hd in range(2):
            acc = acc_ref[b, 2 * p + hd]
            ot.append(acc[0:HEAD_DIM] / acc[HEAD_DIM:HEAD_DIM + 1])
        sl = slice(p * LANES, (p + 1) * LANES)
        o_ref[b, :, sl] = (jnp.concatenate(ot, axis=0).T * _silu(z_ref[b, :, sl])).astype(BF16)


def _moba(qa, ka, vta, kmean, za):
    b, s, _ = qa.shape
    nblk = s // MOBA_BLOCK
    assert nblk <= LANES and nblk % 2 == 0 and MOBA_BLOCK & (MOBA_BLOCK - 1) == 0
    npl = MOBA_PAIRS_PER_STEP
    wl = npl * LANES
    tile = pl.BlockSpec((b, MOBA_BLOCK, wl), lambda p, i: (0, i, p))
    return pl.pallas_call(
        _moba_kernel,
        grid=(MOBA_WIDTH // wl, nblk),
        in_specs=[tile,
                  pl.BlockSpec((b, s, wl), lambda p, i: (0, 0, p)),
                  pl.BlockSpec((b, nblk, 2 * npl, MOBA_VROWS, MOBA_BLOCK), lambda p, i: (0, 0, p, 0, 0)),
                  pl.BlockSpec((b, nblk, wl), lambda p, i: (0, 0, p)),
                  tile],
        out_specs=tile,
        out_shape=jax.ShapeDtypeStruct((b, s, MOBA_WIDTH), BF16),
        scratch_shapes=[pltpu.VMEM((b, 2 * npl, MOBA_VROWS, MOBA_BLOCK), F32)],
        compiler_params=pltpu.CompilerParams(dimension_semantics=("arbitrary",) * 2,
                                             vmem_limit_bytes=VMEM_LIMIT),
        name="moba",
    )(qa, ka, vta, kmean, za)


def _segsum(xs, ones256):
    m = xs[0].shape[0]
    parts = []
    for x in xs:
        for piece in _split2(x):
            parts += [piece[:, :256], piece[:, 256:]]
    res = _dot(jnp.concatenate(parts, axis=0), ones256)
    outs = []
    for idx in range(len(xs)):
        r = [res[(idx * 4 + j) * m:(idx * 4 + j + 1) * m] for j in range(4)]
        outs.append(jnp.concatenate([r[0] + r[2], r[1] + r[3]], axis=1))
    return outs


def _interleave(*gens):
    gens = list(gens)
    while gens:
        for g in list(gens):
            try:
                next(g)
            except StopIteration:
                gens.remove(g)


def _rwkv_pairs(get, v, ptot, s2, masks, out):
    head0, head0_wide, strict, incl, bdiag = masks
    at, rt, kh, bh = get(0), get(1), get(2), get(3)
    c = at[0].shape[0]
    idx = range(len(at))
    lo = lambda t: jnp.where(head0, t, 0.0)
    hi = lambda t: jnp.where(head0, 0.0, t)
    split = lambda t: jnp.concatenate([lo(t), hi(t)], axis=0)
    lhs = [jnp.concatenate([at[i], rt[i]], axis=0).astype(BF16) for i in idx]
    rhs = [jnp.concatenate([split(bh[i]), split(kh[i])], axis=0).astype(BF16) for i in idx]
    sc = [_dot_nt(lhs[i], rhs[i]) for i in idx]
    yield
    p = [jnp.where(strict, sc[i][0:c], 0.0) for i in idx]
    mr = [jnp.where(incl, sc[i][c:2 * c], 0.0).astype(BF16) for i in idx]
    vs = [split(v[i]).astype(BF16) for i in idx]
    xu = [_dot(p[i][:, LANES:].astype(BF16), vs[i]) for i in idx]
    yield
    x = [jnp.concatenate([at[i], xu[i]], axis=1) for i in idx]
    p = [p[i][:, 0:LANES] for i in idx]
    steps = (c - 1).bit_length()
    for j in range(steps):
        pb = [p[i].astype(BF16) for i in idx]
        xs = [jnp.concatenate([jnp.where(head0_wide, x[i], 0.0), jnp.where(head0_wide, 0.0, x[i])],
                              axis=0).astype(BF16) for i in idx]
        x = [x[i] + _dot(pb[i], xs[i]) for i in idx]
        if j + 1 < steps:
            p = [_dot(pb[i], split(p[i]).astype(BF16)) for i in idx]
        yield
    wr = [_dot_nt(jnp.concatenate([x[i][:, 0:LANES], rt[i]], axis=0).astype(BF16), s2[i].astype(BF16))
          for i in idx]
    yield
    u = [wr[i][0:c] + x[i][:, LANES:] for i in idx]
    ym = [_dot(mr[i], jnp.concatenate([split(u[i]), split(v[i])], axis=0).astype(BF16)) for i in idx]
    vu = [jnp.concatenate([u[i], v[i]], axis=0).astype(BF16) for i in idx]
    kt, bt = get(4), get(5)
    bk = [jnp.concatenate([bt[i], kt[i]], axis=0).astype(BF16) for i in idx]
    s2_new = [s2[i] * ptot[i] + jnp.where(bdiag, _dot_tn(vu[i], bk[i]), 0.0) for i in idx]
    yield
    y = [wr[i][c:2 * c] + ym[i] for i in idx]
    out.append((y, s2_new))


def _rwkv_prep(h, vec_ref, wcomb_ref, ones256, nb, slot, prep_ref, ptot_ref):
    m = h.shape[0]
    c = m // nb
    w = RWKV_WIDTH
    r = h[:, 0:w]
    k = h[:, w:2 * w]
    lo = h[:, 3 * w:3 * w + 2 * LORA]
    w0, a0, k_k, k_a, r_k = [vec_ref[j:j + 1, :] for j in range(5)]

    lane_m = lax.broadcasted_iota(jnp.int32, (m, LANES), 1)
    lo = jnp.where(lane_m < LORA, jnp.tanh(lo), lo)
    wa = _dot(lo.astype(BF16), wcomb_ref[...])
    logw = -DECAY_SCALE * jax.nn.sigmoid(w0 + wa[:, 0:w])
    yield
    a = jax.nn.sigmoid(a0 + wa[:, w:])
    kkr = k * k_k
    k2 = k * (1.0 + (a - 1.0) * k_a)
    ss, rk = _segsum([kkr * kkr, r * k2 * r_k], ones256)
    prep_ref[slot, 6] = rk
    yield
    kk = kkr / jnp.maximum(jnp.sqrt(ss), 1e-12)
    bb = kk * a

    tr = lax.broadcasted_iota(jnp.int32, (m, m), 0)
    tc = lax.broadcasted_iota(jnp.int32, (m, m), 1)
    sh = c.bit_length() - 1
    tri = jnp.where(tr >= tc, jnp.where((tr >> sh) == (tc >> sh), 1.0, 0.0), 0.0).astype(BF16)
    cinc = sum(_dot(tri, piece) for piece in _split2(logw))
    yield
    prep_ref[slot, 0] = -kk * jnp.exp(cinc - logw)
    prep_ref[slot, 1] = r * jnp.exp(cinc)
    yield
    einv = jnp.exp(-cinc)
    prep_ref[slot, 2] = k2 * einv
    prep_ref[slot, 3] = bb * einv
    yield
    ctot = jnp.concatenate([jnp.broadcast_to(cinc[(b + 1) * c - 1:(b + 1) * c], (c, w)) for b in range(nb)], axis=0)
    edec = jnp.exp(ctot - cinc)
    prep_ref[slot, 4] = k2 * edec
    prep_ref[slot, 5] = bb * edec
    for b in range(nb):
        ptot_ref[slot, b] = jnp.exp(cinc[(b + 1) * c - 1:(b + 1) * c])


def _rwkv_main(h, nb, slot, prep_ref, ptot_ref, st_ref, masks, out):
    m = h.shape[0]
    c = m // nb
    w = RWKV_WIDTH
    v = h[:, 2 * w:3 * w]
    rk = prep_ref[slot, 6]
    npair = w // LANES
    pairs = [(b, p) for b in range(nb) for p in range(npair)]
    get = lambda j: [prep_ref[slot, j, b * c:(b + 1) * c, p * LANES:(p + 1) * LANES] for b, p in pairs]
    res = []
    yield from _rwkv_pairs(get, [v[b * c:(b + 1) * c, p * LANES:(p + 1) * LANES] for b, p in pairs],
                           [ptot_ref[slot, b, :, p * LANES:(p + 1) * LANES] for b, p in pairs],
                           [st_ref[b, p] for b, p in pairs], masks, res)
    ypairs, s_new = res[0]
    for (b, p), s in zip(pairs, s_new):
        st_ref[b, p] = s
    y = jnp.concatenate([jnp.concatenate(ypairs[b * npair:(b + 1) * npair], axis=1) for b in range(nb)],
                        axis=0)
    out.append((y, rk))


def _rwkv_tail(h, y, rk, vec_ref, ones256, out):
    w = RWKV_WIDTH
    gn_g, gn_b = vec_ref[5:6, :], vec_ref[6:7, :]
    (ysum,) = _segsum([y], ones256)
    yield
    d = y - ysum * (1.0 / HEAD_DIM)
    (dsq,) = _segsum([d * d], ones256)
    yield
    yn = d * lax.rsqrt(dsq * (1.0 / HEAD_DIM) + GN_EPS) * gn_g + gn_b
    ob = yn + rk * h[:, 2 * w:3 * w]
    out.append((ob * _silu(h[:, 3 * w + 2 * LORA:])).astype(BF16))


def _rwkv_kernel(h_ref, hn_ref, vec_ref, wcomb_ref, ones_ref, o_ref, st_ref, prep_ref, ptot_ref):
    step = pl.program_id(0)
    nb = h_ref.shape[0]
    c = RWKV_CHUNK
    ones256 = ones_ref[...]
    flat = lambda t: t.reshape(nb * c, t.shape[2])
    prep = lambda h, slot: _rwkv_prep(h, vec_ref, wcomb_ref, ones256, nb, slot, prep_ref, ptot_ref)

    sh = c.bit_length() - 1
    r2 = lax.broadcasted_iota(jnp.int32, (2 * c, 2 * c), 0)
    c2 = lax.broadcasted_iota(jnp.int32, (2 * c, 2 * c), 1)
    ti = lax.broadcasted_iota(jnp.int32, (c, 4 * c), 0)
    tj = lax.broadcasted_iota(jnp.int32, (c, 4 * c), 1) & (c - 1)
    masks = (lax.broadcasted_iota(jnp.int32, (c, LANES), 1) < HEAD_DIM,
             (lax.broadcasted_iota(jnp.int32, (c, 2 * LANES), 1) & HEAD_DIM) == 0,
             ti > tj, ti >= tj, (r2 >> sh) == (c2 >> sh))
    main = lambda h, slot, out: _rwkv_main(h, nb, slot, prep_ref, ptot_ref, st_ref, masks, out)
    tail = lambda h, yr, out: _rwkv_tail(h, yr[0], yr[1], vec_ref, ones256, out)

    h0 = flat(h_ref[:, 0:c, :])
    h1 = flat(h_ref[:, c:2 * c, :])

    @pl.when(step == 0)
    def _():
        st_ref[...] = jnp.zeros(st_ref.shape, F32)
        _interleave(prep(h0, 0))

    yr0, yr1, u0, u1 = [], [], [], []
    _interleave(main(h0, 0, yr0), prep(h1, 1))
    _interleave(main(h1, 1, yr1), prep(flat(hn_ref[...]), 0), tail(h0, yr0[0], u0))
    _interleave(tail(h1, yr1[0], u1))
    for b in range(nb):
        o_ref[b, 0:c, :] = u0[0][b * c:(b + 1) * c]
        o_ref[b, c:2 * c, :] = u1[0][b * c:(b + 1) * c]


def _rwkv(hb, vecs, wcomb, ones256):
    b, s, wcols = hb.shape
    c = RWKV_CHUNK
    assert c & (c - 1) == 0 and 2 * c == LANES and s % (2 * c) == 0
    nsteps = s // (2 * c)
    full = lambda a: pl.BlockSpec(a.shape, lambda i: (0,) * a.ndim)
    return pl.pallas_call(
        _rwkv_kernel,
        grid=(nsteps,),
        in_specs=[pl.BlockSpec((b, 2 * c, wcols), lambda i: (0, i, 0)),
                  pl.BlockSpec((b, c, wcols), lambda i: (0, jnp.minimum(2 * i + 2, 2 * nsteps - 1), 0)),
                  full(vecs), full(wcomb), full(ones256)],
        out_specs=pl.BlockSpec((b, 2 * c, RWKV_WIDTH), lambda i: (0, i, 0)),
        out_shape=jax.ShapeDtypeStruct((b, s, RWKV_WIDTH), BF16),
        scratch_shapes=[pltpu.VMEM((b, RWKV_WIDTH // LANES, LANES, LANES), F32),
                        pltpu.VMEM((2, 7, b * c, RWKV_WIDTH), F32),
                        pltpu.VMEM((2, b, 1, RWKV_WIDTH), F32)],
        compiler_params=pltpu.CompilerParams(dimension_semantics=("arbitrary",),
                                             vmem_limit_bytes=VMEM_LIMIT),
        name="rwkv",
    )(hb, hb, vecs, wcomb, ones256)


def _swa_kernel(sink_ref, q_ref, kp_ref, kc_ref, vp_ref, vc_ref, z_ref, o_ref):
    n = pl.program_id(0)
    nb, bq, _ = q_ref.shape
    cols = SWA_GROUP * bq
    ngroup = SWA_KV_WIDTH // HEAD_DIM
    chains = [(b, g) for b in range(nb) for g in range(ngroup)]
    cs = range(len(chains))
    qt = [q_ref[b].astype(F32).T for b in range(nb)]
    kband = [jnp.concatenate([kp_ref[b], kc_ref[b]], axis=0) for b in range(nb)]
    vt = [jnp.concatenate([vp_ref[b], vc_ref[b]], axis=0).astype(F32).T.astype(BF16) for b in range(nb)]
    kc = lax.broadcasted_iota(jnp.int32, (2 * bq, cols), 0)
    d = kc - (lax.broadcasted_iota(jnp.int32, (2 * bq, cols), 1) & (bq - 1))
    lo_ok = jnp.where(n > 0, 0, bq)
    d = jnp.where(kc >= lo_ok, d, 0)
    lane = lax.broadcasted_iota(jnp.int32, (1, cols), 1)
    zeros = jnp.zeros((HEAD_DIM, cols), F32)
    sinks = []
    for g in range(ngroup):
        sink = jnp.zeros((1, cols), F32)
        for jj in range(SWA_GROUP):
            sink = jnp.where(lane >= jj * bq, sink_ref[g * SWA_GROUP + jj], sink)
        sinks.append(sink)
    qaug = []
    for b, g in chains:
        qg = jnp.concatenate([qt[b][j * HEAD_DIM:(j + 1) * HEAD_DIM]
                              for j in range(g * SWA_GROUP, (g + 1) * SWA_GROUP)], axis=1)
        qaug.append(jnp.concatenate([qg, zeros] if g == 0 else [zeros, qg], axis=0).astype(BF16))
    st = [_dot(kband[b], qaug[c]) for c, (b, g) in enumerate(chains)]
    st = [jnp.where(d >= 1, jnp.where(d <= bq, st[c], NEG), NEG) for c in cs]
    mx = [jnp.maximum(jnp.max(st[c], axis=0, keepdims=True), sinks[g]) for c, (b, g) in enumerate(chains)]
    p = [jnp.exp(st[c] - mx[c]) for c in cs]
    denom = [jnp.sum(p[c], axis=0, keepdims=True) + jnp.exp(sinks[g] - mx[c]) for c, (b, g) in enumerate(chains)]
    ot = [_dot(vt[b], (p[c] / denom[c]).astype(BF16)) for c, (b, g) in enumerate(chains)]
    for b in range(nb):
        rows = [ot[b * ngroup + g][g * HEAD_DIM:(g + 1) * HEAD_DIM, jj * bq:(jj + 1) * bq]
                for g in range(ngroup) for jj in range(SWA_GROUP)]
        o_ref[b] = (jnp.concatenate(rows, axis=0).T * _silu(z_ref[b])).astype(BF16)


def _swa(sinks, qc, kc, vc, zc):
    b, s, _ = qc.shape
    bq = SWA_BLOCK
    assert bq & (bq - 1) == 0
    cur = lambda w: pl.BlockSpec((b, bq, w), lambda n, sk: (0, n, 0))
    prv = lambda w: pl.BlockSpec((b, bq, w), lambda n, sk: (0, jnp.maximum(n - 1, 0), 0))
    return pl.pallas_call(
        _swa_kernel,
        grid_spec=pltpu.PrefetchScalarGridSpec(
            num_scalar_prefetch=1,
            grid=(s // bq,),
            in_specs=[cur(SWA_WIDTH), prv(SWA_KV_WIDTH), cur(SWA_KV_WIDTH),
                      prv(SWA_KV_WIDTH), cur(SWA_KV_WIDTH), cur(SWA_WIDTH)],
            out_specs=cur(SWA_WIDTH)),
        out_shape=jax.ShapeDtypeStruct((b, s, SWA_WIDTH), BF16),
        compiler_params=pltpu.CompilerParams(dimension_semantics=("arbitrary",),
                                             vmem_limit_bytes=VMEM_LIMIT),
        name="swa",
    )(sinks, qc, kc, kc, vc, vc, zc)


def _out_kernel(x_ref, ua_ref, ub_ref, uc_ref, wg_ref, wa_ref, wb_ref, wc_ref, wo_ref, lng_ref, lnb_ref,
                o_ref, *, alpha):
    x = x_ref[...]
    d = x.shape[1]
    g = _dot(x.astype(BF16), wg_ref[...])
    merged = (jax.nn.sigmoid(g[:, 0:d]) * _dot(ua_ref[...], wa_ref[...])
              + jax.nn.sigmoid(g[:, d:2 * d]) * _dot(ub_ref[...], wb_ref[...])
              + jax.nn.sigmoid(g[:, 2 * d:]) * _dot(uc_ref[...], wc_ref[...]))
    y = alpha * x + _dot(merged.astype(BF16), wo_ref[...])
    mu = jnp.mean(y, axis=-1, keepdims=True)
    dev = y - mu
    var = jnp.mean(dev * dev, axis=-1, keepdims=True)
    o_ref[...] = dev * lax.rsqrt(var + LN_EPS) * lng_ref[...] + lnb_ref[...]


def _out(x2, ua, ub, uc, wg, wa, wb, wc, wo, lng, lnb, alpha, tm=ROW_TILE):
    t, d = x2.shape
    row = lambda w: pl.BlockSpec((tm, w), lambda i: (i, 0))
    full = lambda a: pl.BlockSpec(a.shape, lambda i: (0,) * a.ndim, pipeline_mode=pl.Buffered(1))
    return pl.pallas_call(
        functools.partial(_out_kernel, alpha=alpha),
        grid=(t // tm,),
        in_specs=[row(d), row(ua.shape[1]), row(ub.shape[1]), row(uc.shape[1]),
                  full(wg), full(wa), full(wb), full(wc), full(wo), full(lng), full(lnb)],
        out_specs=row(d),
        out_shape=jax.ShapeDtypeStruct((t, d), F32),
        compiler_params=pltpu.CompilerParams(dimension_semantics=("arbitrary",),
                                             vmem_limit_bytes=VMEM_LIMIT),
        name="merge_out",
    )(x2, ua, ub, uc, wg, wa, wb, wc, wo, lng, lnb)


def kernel(x, positions, w_in, mu_rwkv, w0, w_up, a0, a_up, k_k, k_a, r_k, gn_g, gn_b, sinks,
           w_branch_a, w_branch_b, w_branch_c, w_out, ln_g, ln_b):
    b, s, d = x.shape
    depth = w_in.shape[0]
    t = b * s
    alpha = (2 * depth) ** 0.25
    assert w_in.shape[2] == A_COLS + B_COLS + C_COLS + 3 * d

    half = HEAD_DIM // 2
    inv = jnp.power(ROPE_THETA, -jnp.arange(half, dtype=F32) / half)
    inv_row = jnp.tile(inv, LANES // half).reshape(1, LANES)
    cos, sin = _rope_tables(positions.reshape(t, 1), inv_row)

    head_of_lane = jnp.arange(256) // HEAD_DIM
    ones256 = (head_of_lane[:, None] == head_of_lane[None, :]).astype(BF16)

    x2 = x.reshape(t, d)
    for l in range(depth):
        wl = w_in[l].astype(BF16)
        wa = wl[:, 0:A_COLS]
        wb = wl[:, A_COLS:A_COLS + B_COLS]
        wc = wl[:, A_COLS + B_COLS:A_COLS + B_COLS + C_COLS]
        wg = wl[:, A_COLS + B_COLS + C_COLS:]
        qa, ka, vta, za, kmean, hb, qc, kc, vc, zc = _proj(
            x2, cos, sin, wa, wb, wc, mu_rwkv[l].reshape(1, B_COLS), s)

        nblk = s // MOBA_BLOCK
        ua = _moba(qa.reshape(b, s, MOBA_WIDTH), ka.reshape(b, s, MOBA_WIDTH),
                   vta.reshape(b, nblk, MOBA_WIDTH // HEAD_DIM, MOBA_VROWS, MOBA_BLOCK),
                   kmean.reshape(b, nblk, MOBA_WIDTH), za.reshape(b, s, MOBA_WIDTH))

        zeros_lora = jnp.zeros((LORA, RWKV_WIDTH), F32)
        wcomb = jnp.concatenate([jnp.concatenate([w_up[l], zeros_lora], axis=1),
                                 jnp.concatenate([zeros_lora, a_up[l]], axis=1)], axis=0).astype(BF16)
        vecs = jnp.stack([w0[l], a0[l], k_k[l], k_a[l], r_k[l].reshape(-1), gn_g[l], gn_b[l],
                          jnp.zeros((RWKV_WIDTH,), F32)], axis=0)
        ub = _rwkv(hb.reshape(b, s, B_COLS), vecs, wcomb, ones256)

        uc = _swa(sinks[l], qc.reshape(b, s, SWA_WIDTH), kc.reshape(b, s, SWA_KV_WIDTH),
                  vc.reshape(b, s, SWA_KV_WIDTH), zc.reshape(b, s, SWA_WIDTH))

        x2 = _out(x2, ua.reshape(t, MOBA_WIDTH), ub.reshape(t, RWKV_WIDTH), uc.reshape(t, SWA_WIDTH),
                  wg, w_branch_a[l].astype(BF16), w_branch_b[l].astype(BF16), w_branch_c[l].astype(BF16),
                  w_out[l].astype(BF16), ln_g[l].reshape(1, d), ln_b[l].reshape(1, d), alpha)
    return x2.reshape(b, s, d)
```
